```python
import jax, jax.numpy as jnp
from jax import lax
import numpy as np

D_MODEL = 1024
BATCH = 16
SEQ = 2048
DEPTH = 4

N_MIXERS = 2
N_HGRN_LAYERS = (DEPTH + 1) // 2
N_DSA_LAYERS = DEPTH // 2
N_DENSE_LAYERS = (DEPTH + 1) // 2
N_MOE_LAYERS = DEPTH // 2

HG_HEADS = 8
HG_DK = D_MODEL // HG_HEADS
HG_DV = D_MODEL // HG_HEADS
HG_CHUNK = 64

DSA_HEADS = 8
DSA_HEAD_DIM = 128
DSA_Q_RANK = 256
DSA_KV_RANK = 256
IDX_HEADS = 8
IDX_DIM = 64
IDX_TOPK_MAX = 256
Q_BLOCK = 128

D_FF = 3584
N_EXPERTS = 8
TOP_K_EXPERTS = 2

ALPHA = (2 * DEPTH) ** 0.25
BETA = (8 * DEPTH) ** -0.25
LN_EPS = 1e-5
RMS_EPS = 1e-6
NEG_BIG = -1e30

kernel_name = 'hgrn2_dsa_moe_deepnorm_hybrid'


def layer_norm(x, g, b):
    xf = x.astype(jnp.float32)
    mu = jnp.mean(xf, axis=-1, keepdims=True)
    var = jnp.mean(jnp.square(xf - mu), axis=-1, keepdims=True)
    return ((xf - mu) * lax.rsqrt(var + LN_EPS) * g + b).astype(x.dtype)


def rms_norm(x, g):
    xf = x.astype(jnp.float32)
    ms = jnp.mean(jnp.square(xf), axis=-1, keepdims=True)
    return (xf * lax.rsqrt(ms + RMS_EPS) * g).astype(x.dtype)


def swiglu(h, w_gate_up, w_down):
    gate, up = jnp.split(h @ w_gate_up, 2, axis=-1)
    return (jax.nn.silu(gate) * up) @ w_down


def hgrn2_mixer(x, w_in, lb, norm_g, w_out):
    B, S, _ = x.shape
    nc = S // HG_CHUNK
    q, f_pre, inp, g = jnp.split(x @ w_in, 4, axis=-1)
    f_pre = f_pre.astype(jnp.float32)
    f = lb + (1.0 - lb) * jax.nn.sigmoid(f_pre)
    log_f = jnp.log(f)
    k = (1.0 - lb) * jax.nn.sigmoid(-f_pre)

    def to_chunks(t, d):
        return t.astype(jnp.float32).reshape(B, nc, HG_CHUNK, HG_HEADS, d).transpose(1, 0, 3, 2, 4)

    qc, kc, lfc = to_chunks(q, HG_DK), to_chunks(k, HG_DK), to_chunks(log_f, HG_DK)
    vc = to_chunks(inp, HG_DV)
    causal = jnp.tril(jnp.ones((HG_CHUNK, HG_CHUNK), dtype=bool))[:, :, None]

    def step(state, chunk):
        q_, k_, v_, lf_ = chunk
        b = jnp.cumsum(lf_, axis=2)
        diff = b[:, :, :, None, :] - b[:, :, None, :, :]
        decay = jnp.where(causal, jnp.exp(jnp.where(causal, diff, 0.0)), 0.0)
        scores = jnp.einsum('bhtd,bhtsd,bhsd->bhts', q_, decay, k_)
        o = (jnp.einsum('bhts,bhsv->bhtv', scores, v_)
             + jnp.einsum('bhtd,bhdv->bhtv', q_ * jnp.exp(b), state))
        b_last = b[:, :, -1:, :]
        state = (jnp.exp(b_last[:, :, 0, :])[..., None] * state
                 + jnp.einsum('bhsd,bhsv->bhdv', k_ * jnp.exp(b_last - b), v_))
        return state, o

    state0 = jnp.zeros((B, HG_HEADS, HG_DK, HG_DV), jnp.float32)
    _, o = lax.scan(step, state0, (qc, kc, vc, lfc))
    o = o.transpose(1, 0, 3, 2, 4).reshape(B, S, HG_HEADS, HG_DV)
    o = rms_norm(o, norm_g) * jax.nn.silu(g.astype(jnp.float32)).reshape(B, S, HG_HEADS, HG_DV)
    return o.astype(x.dtype).reshape(B, S, D_MODEL) @ w_out


def dsa_mixer(x, w_in, q_norm_g, kv_norm_g, w_uq, w_uk, w_uv, w_qidx, kidx_norm_g, kidx_norm_b, w_out):
    B, S, _ = x.shape
    topk = min(IDX_TOPK_MAX, S // 4)
    splits = [DSA_Q_RANK, DSA_Q_RANK + DSA_KV_RANK, DSA_Q_RANK + DSA_KV_RANK + IDX_DIM]
    c_q, c_kv, k_idx, w_idx = jnp.split(x @ w_in, splits, axis=-1)
    c_q = rms_norm(c_q, q_norm_g)
    c_kv = rms_norm(c_kv, kv_norm_g)
    q = (c_q @ w_uq).reshape(B, S, DSA_HEADS, DSA_HEAD_DIM)
    q_lat = jnp.einsum('bshd,hdc->bshc', q, w_uk)
    q_idx = (c_q @ w_qidx).reshape(B, S, IDX_HEADS, IDX_DIM)
    k_idx = layer_norm(k_idx, kidx_norm_g, kidx_norm_b)
    w_idx = w_idx.astype(jnp.float32) * (IDX_HEADS ** -0.5 * IDX_DIM ** -0.5)
    key_pos = jnp.arange(S)

    def block(blk):
        t0 = blk * Q_BLOCK
        sl = lambda t: lax.dynamic_slice_in_dim(t, t0, Q_BLOCK, axis=1)
        qi, wi, ql = sl(q_idx), sl(w_idx), sl(q_lat)
        q_pos = t0 + jnp.arange(Q_BLOCK)
        rel = jax.nn.relu(jnp.einsum('bthd,bsd->bths', qi, k_idx).astype(jnp.float32))
        idx_score = jnp.einsum('bths,bth->bts', rel, wi)
        causal = key_pos[None, :] <= q_pos[:, None]
        idx_score = jnp.where(causal[None], idx_score, NEG_BIG)
        _, sel = lax.top_k(idx_score, topk)
        valid = sel <= q_pos[None, :, None]
        kv_sel = jax.vmap(lambda c, s: c[s])(c_kv, sel)
        logits = jnp.einsum('bthc,btkc->bthk', ql, kv_sel).astype(jnp.float32) * (DSA_HEAD_DIM ** -0.5)
        logits = jnp.where(valid[:, :, None, :], logits, NEG_BIG)
        p = jax.nn.softmax(logits, axis=-1).astype(kv_sel.dtype)
        return jnp.einsum('bthk,btkc->bthc', p, kv_sel)

    o_lat = lax.map(block, jnp.arange(S // Q_BLOCK))
    o_lat = jnp.moveaxis(o_lat, 0, 1).reshape(B, S, DSA_HEADS, DSA_KV_RANK)
    o = jnp.einsum('bshc,hcd->bshd', o_lat, w_uv).reshape(B, S, DSA_HEADS * DSA_HEAD_DIM)
    return o @ w_out


def moe_swiglu(h, w_router, w_gate_up, w_down):
    logits = (h @ w_router).astype(jnp.float32)
    top_vals, top_idx = lax.top_k(logits, TOP_K_EXPERTS)
    top_w = jax.nn.softmax(top_vals, axis=-1)
    gates = jnp.sum(jax.nn.one_hot(top_idx, N_EXPERTS, dtype=jnp.float32) * top_w[..., None], axis=-2)
    y = jnp.zeros_like(h)
    for e in range(N_EXPERTS):
        y = y + gates[..., e:e + 1].astype(h.dtype) * swiglu(h, w_gate_up[e], w_down[e])
    return y


def setup_inputs(seed: int = 0) -> dict:
    key = jax.random.key(seed)
    ks = iter(jax.random.split(key, 32))

    def w(shape, fan_in, scale=1.0):
        return jax.random.normal(next(ks), shape, jnp.float32) * (scale * fan_in ** -0.5)

    def near_one(shape):
        return 1.0 + 0.02 * jax.random.normal(next(ks), shape, jnp.float32)

    def small(shape, s=0.02):
        return s * jax.random.normal(next(ks), shape, jnp.float32)

    dsa_in_width = DSA_Q_RANK + DSA_KV_RANK + IDX_DIM + IDX_HEADS
    x = jax.random.normal(next(ks), (BATCH, SEQ, D_MODEL), jnp.float32)
    return {
        'x': x,
        'ln_g': near_one((DEPTH, 2, D_MODEL)),
        'ln_b': small((DEPTH, 2, D_MODEL)),
        'hg_w_in': w((N_HGRN_LAYERS, D_MODEL, 4 * D_MODEL), D_MODEL),
        'hg_lower_bounds': small((N_HGRN_LAYERS, D_MODEL), 0.5),
        'hg_norm_g': near_one((N_HGRN_LAYERS, HG_HEADS, HG_DV)),
        'hg_w_out': w((N_HGRN_LAYERS, D_MODEL, D_MODEL), D_MODEL, BETA),
        'dsa_w_in': w((N_DSA_LAYERS, D_MODEL, dsa_in_width), D_MODEL),
        'dsa_q_norm_g': near_one((N_DSA_LAYERS, DSA_Q_RANK)),
        'dsa_kv_norm_g': near_one((N_DSA_LAYERS, DSA_KV_RANK)),
        'dsa_w_uq': w((N_DSA_LAYERS, DSA_Q_RANK, DSA_HEADS * DSA_HEAD_DIM), DSA_Q_RANK),
        'dsa_w_uk': w((N_DSA_LAYERS, DSA_HEADS, DSA_HEAD_DIM, DSA_KV_RANK), DSA_HEAD_DIM),
        'dsa_w_uv': w((N_DSA_LAYERS, DSA_HEADS, DSA_KV_RANK, DSA_HEAD_DIM), DSA_KV_RANK),
        'dsa_w_qidx': w((N_DSA_LAYERS, DSA_Q_RANK, IDX_HEADS * IDX_DIM), DSA_Q_RANK),
        'dsa_kidx_norm_g': near_one((N_DSA_LAYERS, IDX_DIM)),
        'dsa_kidx_norm_b': small((N_DSA_LAYERS, IDX_DIM)),
        'dsa_w_out': w((N_DSA_LAYERS, DSA_HEADS * DSA_HEAD_DIM, D_MODEL), DSA_HEADS * DSA_HEAD_DIM, BETA),
        'ffn_w_gate_up': w((N_DENSE_LAYERS, D_MODEL, 2 * D_FF), D_MODEL),
        'ffn_w_down': w((N_DENSE_LAYERS, D_FF, D_MODEL), D_FF, BETA),
        'moe_w_router': w((N_MOE_LAYERS, D_MODEL, N_EXPERTS), D_MODEL),
        'moe_w_gate_up': w((N_MOE_LAYERS, N_EXPERTS, D_MODEL, 2 * D_FF), D_MODEL),
        'moe_w_down': w((N_MOE_LAYERS, N_EXPERTS, D_FF, D_MODEL), D_FF, BETA),
    }


def reference(x, ln_g, ln_b, hg_w_in, hg_lower_bounds, hg_norm_g, hg_w_out, dsa_w_in, dsa_q_norm_g,
              dsa_kv_norm_g, dsa_w_uq, dsa_w_uk, dsa_w_uv, dsa_w_qidx, dsa_kidx_norm_g, dsa_kidx_norm_b,
              dsa_w_out, ffn_w_gate_up, ffn_w_down, moe_w_router, moe_w_gate_up, moe_w_down):
    lb_all = jax.nn.softmax(hg_lower_bounds.astype(jnp.float32), axis=0)
    lb_all = jnp.cumsum(lb_all, axis=0) - lb_all[0]
    for layer in range(DEPTH):
        j = layer // N_MIXERS
        if layer % N_MIXERS == 0:
            mix = hgrn2_mixer(x, hg_w_in[j], lb_all[j], hg_norm_g[j], hg_w_out[j])
        else:
            mix = dsa_mixer(x, dsa_w_in[j], dsa_q_norm_g[j], dsa_kv_norm_g[j], dsa_w_uq[j], dsa_w_uk[j],
                            dsa_w_uv[j], dsa_w_qidx[j], dsa_kidx_norm_g[j], dsa_kidx_norm_b[j], dsa_w_out[j])
        x = layer_norm(ALPHA * x + mix, ln_g[layer, 0], ln_b[layer, 0])
        m = layer // 2
        if layer % 2 == 0:
            ff = swiglu(x, ffn_w_gate_up[m], ffn_w_down[m])
        else:
            ff = moe_swiglu(x, moe_w_router[m], moe_w_gate_up[m], moe_w_down[m])
        x = layer_norm(ALPHA * x + ff, ln_g[layer, 1], ln_b[layer, 1])
    return x
```

```python
import functools

import numpy as np
import jax
import jax.numpy as jnp
from jax import lax
from jax.experimental import pallas as pl
from jax.experimental.pallas import tpu as pltpu

F32, BF16, I32 = jnp.float32, jnp.bfloat16, jnp.int32

HG_HEADS = 8
DSA_HEADS = 8
DSA_HEAD_DIM = 128
DSA_Q_RANK = 256
DSA_KV_RANK = 256
IDX_HEADS = 8
IDX_DIM = 64
IDX_TOPK_MAX = 256
N_EXPERTS = 8
LN_EPS = 1e-5
RMS_EPS = 1e-6
NEG_BIG = -1e30

LANES = 128
V7X_VMEM_BYTES = 64 * 1024 * 1024
VMEM_LIMIT = V7X_VMEM_BYTES * 7 // 8

ROW_TILE = 512
FFN_ROW_TILE = 1024
FFN_F_TILE = 512
MOE_ROW_TILE = 512
HG_TIME_TILE = 512
HG_CHUNK = 128
DSA_Q_TILE = 256
GATHER_TILE = 512


def _cparams(*sem):
    return pltpu.CompilerParams(dimension_semantics=sem, vmem_limit_bytes=VMEM_LIMIT)


def _tile(n, t):
    t = min(n, t)
    assert n % t == 0, (n, t)
    return t


def _dot(a, b):
    return jnp.dot(a, b, preferred_element_type=F32)


def _dot_nt(a, b):
    return lax.dot_general(a, b, (((1,), (1,)), ((), ())), preferred_element_type=F32)


def _dot_tn(a, b):
    return lax.dot_general(a, b, (((0,), (0,)), ((), ())), preferred_element_type=F32)


def _layer_norm(y, g, b):
    mu = jnp.mean(y, axis=-1, keepdims=True)
    yc = y - mu
    var = jnp.mean(yc * yc, axis=-1, keepdims=True)
    return yc * lax.rsqrt(var + LN_EPS) * g + b


def _row_spec(t, d):
    return pl.BlockSpec((t, d), lambda i: (i, 0))


def _const_spec(shape):
    return pl.BlockSpec(shape, lambda *_: (0,) * len(shape))


def _hgrn_proj_kernel(x_ref, w_ref, lb_ref, q_ref, k_ref, lf_ref, v_ref, g_ref):
    d = x_ref.shape[1]
    xb = x_ref[...].astype(BF16)
    lb = lb_ref[...]
    q_ref[...] = _dot(xb, w_ref[:, 0:d])
    fp = _dot(xb, w_ref[:, d:2 * d])
    lf_ref[...] = jnp.log(lb + (1.0 - lb) * jax.nn.sigmoid(fp))
    k_ref[...] = (1.0 - lb) * jax.nn.sigmoid(-fp)
    v_ref[...] = _dot(xb, w_ref[:, 2 * d:3 * d]).astype(BF16)
    g_ref[...] = _dot(xb, w_ref[:, 3 * d:4 * d])


def _hgrn_proj(x, w_in, lb):
    n, d = x.shape
    tm = _tile(n, ROW_TILE)
    outs = [jax.ShapeDtypeStruct((n, d), dt) for dt in (F32, F32, F32, BF16, F32)]
    return pl.pallas_call(
        _hgrn_proj_kernel,
        grid=(n // tm,),
        in_specs=[_row_spec(tm, d), _const_spec((d, 4 * d)), _const_spec((1, d))],
        out_specs=[_row_spec(tm, d)] * 5,
        out_shape=outs,
        compiler_params=_cparams("parallel"),
        name="hgrn_proj",
    )(x, w_in, lb)


def _cumsum_rows(x):
    c = x.shape[0]
    row = lax.broadcasted_iota(I32, x.shape, 0)
    sh = 1
    while sh < min(c, 8):
        x = x + jnp.where(row >= sh, pltpu.roll(x, sh, axis=0), 0.0)
        sh *= 2
    while sh < c:
        x = x + jnp.concatenate([jnp.zeros((sh, x.shape[1]), x.dtype), x[:c - sh, :]], axis=0)
        sh *= 2
    return x


def _level_decay(b, lf, m):
    c = b.shape[0]
    if m >= 4:
        parts = []
        for lo in range(0, c, 2 * m):
            parts.append(b[lo:lo + 2 * m, :] - b[lo + m - 1:lo + m, :])
        d = parts[0] if len(parts) == 1 else jnp.concatenate(parts, axis=0)
    else:
        row = lax.broadcasted_iota(I32, b.shape, 0)
        if m == 1:
            d = jnp.where((row & 1) == 0, 0.0, lf)
        else:
            ph = row & 3
            nxt = pltpu.roll(lf, c - 1, axis=0)
            prv = pltpu.roll(lf, 1, axis=0)
            d = jnp.where(ph == 0, nxt, jnp.where(ph == 1, 0.0, jnp.where(ph == 2, lf, lf + prv)))
    return jnp.exp(-jnp.abs(d))


def _hgrn_levels(chunk):
    levels = []
    m = chunk // 2
    while m >= 1:
        levels.append(m)
        m //= 2
    return levels


def _hgrn_masks(chunk):
    t = np.arange(chunk)[:, None]
    s = np.arange(chunk)[None, :]
    x = t ^ s
    masks = [((t > s) & (x >= m) & (x < 2 * m)) for m in _hgrn_levels(chunk)]
    masks.append(t == s)
    return jnp.asarray(np.stack(masks).astype(np.float32))


def _hgrn_rec_kernel(q_ref, k_ref, lf_ref, v_ref, g_ref, ng_ref, masks_ref, o_ref, st_ref, *, chunk):
    h = pl.program_id(1)
    levels = _hgrn_levels(chunk)

    @pl.when(pl.program_id(2) == 0)
    def _():
        st_ref[...] = jnp.zeros_like(st_ref)

    ng = ng_ref[pl.ds(h, 1), :]

    def body(c, carry):
        sl = pl.ds(pl.multiple_of(c * chunk, chunk), chunk)
        q, k, lf, v, g = q_ref[sl, :], k_ref[sl, :], lf_ref[sl, :], v_ref[sl, :], g_ref[sl, :]
        b = _cumsum_rows(lf)
        s = masks_ref[len(levels)] * _dot_nt(q.astype(BF16), k.astype(BF16))
        for li, m in enumerate(levels):
            e = _level_decay(b, lf, m)
            s = s + masks_ref[li] * _dot_nt((q * e).astype(BF16), (k * e).astype(BF16))
        o = _dot(s.astype(BF16), v)
        st = st_ref[...]
        b_last = b[chunk - 1:chunk, :]
        o = o + _dot_nt((q * jnp.exp(b)).astype(BF16), st.astype(BF16))
        kt = (k * jnp.exp(b_last - b)).astype(BF16)
        st_ref[...] = st * jnp.exp(b_last) + _dot_tn(v, kt)
        ms = jnp.mean(o * o, axis=-1, keepdims=True)
        o = o * lax.rsqrt(ms + RMS_EPS) * ng
        o_ref[sl, :] = (o * (g * jax.nn.sigmoid(g))).astype(o_ref.dtype)
        return carry

    lax.fori_loop(0, q_ref.shape[0] // chunk, body, 0)


def _hgrn_rec(q, k, lf, v, g, norm_g, batch, seq):
    n, d = q.shape
    dh = d // HG_HEADS
    tt = _tile(seq, HG_TIME_TILE)
    chunk = _tile(tt, HG_CHUNK)
    nt = seq // tt
    masks = _hgrn_masks(chunk)
    blk = pl.BlockSpec((tt, dh), lambda b, h, t: (b * nt + t, h))
    return pl.pallas_call(
        functools.partial(_hgrn_rec_kernel, chunk=chunk),
        grid=(batch, HG_HEADS, nt),
        in_specs=[blk, blk, blk, blk, blk, _const_spec(norm_g.shape), _const_spec(masks.shape)],
        out_specs=blk,
        out_shape=jax.ShapeDtypeStruct((n, d), BF16),
        scratch_shapes=[pltpu.VMEM((dh, dh), F32)],
        compiler_params=_cparams("parallel", "parallel", "arbitrary"),
        name="hgrn_rec",
    )(q, k, lf, v, g, norm_g, masks)


def _proj_ln_kernel(a_ref, w_ref, x_ref, g_ref, b_ref, o_ref, *, alpha):
    y = _dot(a_ref[...], w_ref[...])
    o_ref[...] = _layer_norm(alpha * x_ref[...] + y, g_ref[...], b_ref[...])


def _proj_ln(a, w, x, g, b, alpha):
    n, d = x.shape
    ka = a.shape[1]
    tm = _tile(n, ROW_TILE)
    return pl.pallas_call(
        functools.partial(_proj_ln_kernel, alpha=alpha),
        grid=(n // tm,),
        in_specs=[_row_spec(tm, ka), _const_spec((ka, d)), _row_spec(tm, d), _const_spec((1, d)),
                  _const_spec((1, d))],
        out_specs=_row_spec(tm, d),
        out_shape=jax.ShapeDtypeStruct((n, d), F32),
        compiler_params=_cparams("parallel"),
        name="proj_ln",
    )(a, w, x, g, b)


def _swiglu_step(xb, wg_ref, wu_ref, wd_ref):
    gate = _dot(xb, wg_ref[...])
    up = _dot(xb, wu_ref[...])
    act = (gate * jax.nn.sigmoid(gate) * up).astype(BF16)
    return _dot(act, wd_ref[...])


def _ffn_ln_kernel(x_ref, wg_ref, wu_ref, wd_ref, g_ref, b_ref, o_ref, acc_ref, *, alpha):
    f = pl.program_id(1)
    y = _swiglu_step(x_ref[...].astype(BF16), wg_ref, wu_ref, wd_ref)

    @pl.when(f == 0)
    def _():
        acc_ref[...] = y

    @pl.when(f > 0)
    def _():
        acc_ref[...] += y

    @pl.when(f == pl.num_programs(1) - 1)
    def _():
        o_ref[...] = _layer_norm(alpha * x_ref[...] + acc_ref[...], g_ref[...], b_ref[...])


def _ffn_ln(x, w_gate_up, w_down, g, b, alpha):
    n, d = x.shape
    ff = w_down.shape[0]
    tm = _tile(n, FFN_ROW_TILE)
    tf = _tile(ff, FFN_F_TILE)
    nf = ff // tf
    return pl.pallas_call(
        functools.partial(_ffn_ln_kernel, alpha=alpha),
        grid=(n // tm, nf),
        in_specs=[pl.BlockSpec((tm, d), lambda i, f: (i, 0)),
                  pl.BlockSpec((d, tf), lambda i, f: (0, f)),
                  pl.BlockSpec((d, tf), lambda i, f: (0, f + nf)),
                  pl.BlockSpec((tf, d), lambda i, f: (f, 0)),
                  _const_spec((1, d)), _const_spec((1, d))],
        out_specs=pl.BlockSpec((tm, d), lambda i, f: (i, 0)),
        out_shape=jax.ShapeDtypeStruct((n, d), F32),
        scratch_shapes=[pltpu.VMEM((tm, d), F32)],
        compiler_params=_cparams("parallel", "arbitrary"),
        name="ffn_ln",
    )(x, w_gate_up, w_gate_up, w_down, g, b)


def _moe_ffn_kernel(te_ref, nt_ref, x_ref, gate_ref, wg_ref, wu_ref, wd_ref, o_ref, acc_ref):
    i, f = pl.program_id(0), pl.program_id(1)
    live = i < nt_ref[0]

    @pl.when(live)
    def _():
        y = _swiglu_step(x_ref[...].astype(BF16), wg_ref, wu_ref, wd_ref)

        @pl.when(f == 0)
        def _():
            acc_ref[...] = y

        @pl.when(f > 0)
        def _():
            acc_ref[...] += y

    @pl.when(f == pl.num_programs(1) - 1)
    def _():
        d = o_ref.shape[1]
        gate = jnp.concatenate([gate_ref[...]] * (d // LANES), axis=1)
        o_ref[...] = jnp.where(live, acc_ref[...] * gate, 0.0)


def _moe_ffn(tile_expert, n_live, xs, gate_rep, w_gate_up, w_down, tm):
    p, d = xs.shape
    ff = w_down.shape[1]
    assert p % tm == 0
    tf = _tile(ff, FFN_F_TILE)
    nf = ff // tf
    grid_spec = pltpu.PrefetchScalarGridSpec(
        num_scalar_prefetch=2,
        grid=(p // tm, nf),
        in_specs=[pl.BlockSpec((tm, d), lambda i, f, te, nt: (i, 0)),
                  pl.BlockSpec((tm, LANES), lambda i, f, te, nt: (i, 0)),
                  pl.BlockSpec((None, d, tf), lambda i, f, te, nt: (te[i], 0, f)),
                  pl.BlockSpec((None, d, tf), lambda i, f, te, nt: (te[i], 0, f + nf)),
                  pl.BlockSpec((None, tf, d), lambda i, f, te, nt: (te[i], f, 0))],
        out_specs=pl.BlockSpec((tm, d), lambda i, f, te, nt: (i, 0)),
        scratch_shapes=[pltpu.VMEM((tm, d), F32)],
    )
    return pl.pallas_call(
        _moe_ffn_kernel,
        grid_spec=grid_spec,
        out_shape=jax.ShapeDtypeStruct((p, d), F32),
        compiler_params=_cparams("parallel", "arbitrary"),
        name="moe_ffn",
    )(tile_expert, n_live, xs, gate_rep, w_gate_up, w_gate_up, w_down)


def _router_kernel(x_ref, w_ref, gate_ref, idx_ref):
    logits = jnp.dot(x_ref[...], w_ref[...], preferred_element_type=F32, precision=lax.Precision.HIGHEST)
    lane = lax.broadcasted_iota(I32, logits.shape, 1)
    l1 = jnp.where(lane < N_EXPERTS, logits, -jnp.inf)
    m1 = jnp.max(l1, axis=-1, keepdims=True)
    i1 = jnp.min(jnp.where(l1 == m1, lane, LANES), axis=-1, keepdims=True)
    l2 = jnp.where(lane == i1, -jnp.inf, l1)
    m2 = jnp.max(l2, axis=-1, keepdims=True)
    i2 = jnp.min(jnp.where(l2 == m2, lane, LANES), axis=-1, keepdims=True)
    e = jnp.exp(m2 - m1)
    den = 1.0 + e
    gate_ref[...] = jnp.where(lane == 0, 1.0 / den, jnp.where(lane == 1, e / den, 0.0))
    idx_ref[...] = jnp.where(lane == 0, i1, jnp.where(lane == 1, i2, 0))


def _router(x, w_router_pad):
    n, d = x.shape
    tm = _tile(n, ROW_TILE)
    return pl.pallas_call(
        _router_kernel,
        grid=(n // tm,),
        in_specs=[_row_spec(tm, d), _const_spec((d, LANES))],
        out_specs=[_row_spec(tm, LANES)] * 2,
        out_shape=[jax.ShapeDtypeStruct((n, LANES), F32), jax.ShapeDtypeStruct((n, LANES), I32)],
        compiler_params=_cparams("parallel"),
        name="router",
    )(x, w_router_pad)


def _row_copy(src_hbm, row, dst_ref, slot, sem):
    return pltpu.make_async_copy(src_hbm.at[pl.ds(row, 1), :], dst_ref.at[pl.ds(slot, 1), :], sem)


def _gather_rows_kernel(tok_ref, x_hbm, o_ref, sem):
    tg = o_ref.shape[0]
    base = pl.program_id(0) * tg

    def issue(r, c):
        _row_copy(x_hbm, tok_ref[base + r], o_ref, r, sem).start()
        return c

    def drain(r, c):
        _row_copy(x_hbm, 0, o_ref, r, sem).wait()
        return c

    lax.fori_loop(0, tg, issue, 0)
    lax.fori_loop(0, tg, drain, 0)


def _gather_rows(row_tok, x):
    p = row_tok.shape[0]
    d = x.shape[1]
    tg = _tile(p, GATHER_TILE)
    grid_spec = pltpu.PrefetchScalarGridSpec(
        num_scalar_prefetch=1,
        grid=(p // tg,),
        in_specs=[pl.BlockSpec(memory_space=pl.ANY)],
        out_specs=pl.BlockSpec((tg, d), lambda i, tok: (i, 0)),
        scratch_shapes=[pltpu.SemaphoreType.DMA(())],
    )
    return pl.pallas_call(
        _gather_rows_kernel,
        grid_spec=grid_spec,
        out_shape=jax.ShapeDtypeStruct((p, d), x.dtype),
        compiler_params=_cparams("arbitrary"),
        name="gather_rows",
    )(row_tok, x)


def _moe_combine_ln_kernel(p1_ref, p2_ref, x_ref, ys_hbm, g_ref, b_ref, o_ref, buf_ref, sem, *, alpha):
    tc = o_ref.shape[0]
    base = pl.program_id(0) * tc

    def issue(r, c):
        _row_copy(ys_hbm, p1_ref[base + r], buf_ref.at[0], r, sem).start()
        _row_copy(ys_hbm, p2_ref[base + r], buf_ref.at[1], r, sem).start()
        return c

    def drain(r, c):
        _row_copy(ys_hbm, 0, buf_ref.at[0], r, sem).wait()
        _row_copy(ys_hbm, 0, buf_ref.at[1], r, sem).wait()
        return c

    lax.fori_loop(0, tc, issue, 0)
    lax.fori_loop(0, tc, drain, 0)
    y = buf_ref[0] + buf_ref[1]
    o_ref[...] = _layer_norm(alpha * x_ref[...] + y, g_ref[...], b_ref[...])


def _moe_combine_ln(p1, p2, x, ys, g, b, alpha):
    n, d = x.shape
    tc = _tile(n, GATHER_TILE)
    grid_spec = pltpu.PrefetchScalarGridSpec(
        num_scalar_prefetch=2,
        grid=(n // tc,),
        in_specs=[pl.BlockSpec((tc, d), lambda i, a, c: (i, 0)),
                  pl.BlockSpec(memory_space=pl.ANY),
                  pl.BlockSpec((1, d), lambda i, a, c: (0, 0)),
                  pl.BlockSpec((1, d), lambda i, a, c: (0, 0))],
        out_specs=pl.BlockSpec((tc, d), lambda i, a, c: (i, 0)),
        scratch_shapes=[pltpu.VMEM((2, tc, d), F32), pltpu.SemaphoreType.DMA(())],
    )
    return pl.pallas_call(
        functools.partial(_moe_combine_ln_kernel, alpha=alpha),
        grid_spec=grid_spec,
        out_shape=jax.ShapeDtypeStruct((n, d), F32),
        compiler_params=_cparams("arbitrary"),
        name="moe_combine_ln",
    )(p1, p2, x, ys, g, b)


def _moe_routing_tables(idx, gates, tm):
    n = idx.shape[0]
    p_rows = 2 * n + N_EXPERTS * tm
    e_flat = jnp.concatenate([idx[:, 0], idx[:, 1]])
    w_flat = jnp.concatenate([gates[:, 0], gates[:, 1]])
    tok_flat = jnp.concatenate([jnp.arange(n, dtype=I32)] * 2)
    order = jnp.argsort(e_flat, stable=True).astype(I32)
    e_sorted = e_flat[order]
    counts = jnp.sum(e_flat[:, None] == jnp.arange(N_EXPERTS, dtype=I32)[None, :], axis=0).astype(I32)
    starts = jnp.cumsum(counts) - counts
    padded = ((counts + tm - 1) // tm) * tm
    pstarts = jnp.cumsum(padded) - padded
    pos_sorted = pstarts[e_sorted] + jnp.arange(2 * n, dtype=I32) - starts[e_sorted]
    row_tok = jnp.zeros((p_rows,), I32).at[pos_sorted].set(tok_flat[order])
    row_gate = jnp.zeros((p_rows,), F32).at[pos_sorted].set(w_flat[order])
    slot_pos = jnp.zeros((2 * n,), I32).at[order].set(pos_sorted)
    n_tiles = p_rows // tm
    tile_start = jnp.arange(n_tiles, dtype=I32) * tm
    pends = jnp.cumsum(padded)
    tile_expert = jnp.minimum(jnp.sum(tile_start[:, None] >= pends[None, :], axis=1), N_EXPERTS - 1).astype(I32)
    n_live = (pends[-1] // tm).astype(I32).reshape(1)
    return row_tok, row_gate, slot_pos[:n], slot_pos[n:], tile_expert, n_live


def _moe_layer(x, w_router_pad, w_gate_up, w_down, g, b, alpha):
    gate_full, idx_full = _router(x, w_router_pad)
    tm = min(MOE_ROW_TILE, 2 * x.shape[0])
    row_tok, row_gate, p1, p2, tile_expert, n_live = _moe_routing_tables(idx_full[:, :2], gate_full[:, :2], tm)
    xs = _gather_rows(row_tok, x)
    gate_rep = jnp.broadcast_to(row_gate[:, None], (row_gate.shape[0], LANES))
    ys = _moe_ffn(tile_expert, n_live, xs, gate_rep, w_gate_up, w_down, tm)
    return _moe_combine_ln(p1, p2, x, ys, g, b, alpha)


def _rms_norm(x, g):
    ms = jnp.mean(x * x, axis=-1, keepdims=True)
    return x * lax.rsqrt(ms + RMS_EPS) * g


def _dsa_proj_kernel(x_ref, win_ref, qg_ref, kvg_ref, kig_ref, kib_ref, wuq_ref, wuk_ref, wqi_ref,
                     qlat_ref, qidx_ref, ckv_ref, misc_ref, *, w_scale):
    xb = x_ref[...].astype(BF16)
    qr, kr = DSA_Q_RANK, DSA_KV_RANK
    cq = _rms_norm(_dot(xb, win_ref[:, 0:qr]), qg_ref[...]).astype(BF16)
    ckv_ref[...] = _rms_norm(_dot(xb, win_ref[:, qr:qr + kr]), kvg_ref[...]).astype(ckv_ref.dtype)
    misc = _dot(xb, win_ref[:, qr + kr:qr + kr + LANES])
    lane = lax.broadcasted_iota(I32, misc.shape, 1)
    is_key = lane < IDX_DIM
    mu = jnp.sum(jnp.where(is_key, misc, 0.0), axis=-1, keepdims=True) * (1.0 / IDX_DIM)
    kc = jnp.where(is_key, misc - mu, 0.0)
    var = jnp.sum(kc * kc, axis=-1, keepdims=True) * (1.0 / IDX_DIM)
    kn = kc * lax.rsqrt(var + LN_EPS) * kig_ref[...] + kib_ref[...]
    misc_ref[...] = jnp.where(is_key, kn, misc * w_scale)
    qidx_ref[...] = _dot(cq, wqi_ref[...]).astype(qidx_ref.dtype)
    q = _dot(cq, wuq_ref[...])
    hd = DSA_HEAD_DIM
    for h in range(DSA_HEADS):
        qlat_ref[:, h * kr:(h + 1) * kr] = _dot(q[:, h * hd:(h + 1) * hd].astype(BF16), wuk_ref[h]).astype(
            qlat_ref.dtype)


def _dsa_proj(x, w_in_pad, q_g, kv_g, ki_g, ki_b, w_uq, w_uk, w_qidx_pad):
    n, d = x.shape
    tm = _tile(n, ROW_TILE)
    w_scale = IDX_HEADS ** -0.5 * IDX_DIM ** -0.5
    outs = [jax.ShapeDtypeStruct((n, DSA_HEADS * DSA_KV_RANK), BF16),
            jax.ShapeDtypeStruct((n, IDX_HEADS * LANES), BF16),
            jax.ShapeDtypeStruct((n, DSA_KV_RANK), BF16),
            jax.ShapeDtypeStruct((n, LANES), F32)]
    ins = [x, w_in_pad, q_g, kv_g, ki_g, ki_b, w_uq, w_uk, w_qidx_pad]
    return pl.pallas_call(
        functools.partial(_dsa_proj_kernel, w_scale=w_scale),
        grid=(n // tm,),
        in_specs=[_row_spec(tm, d)] + [_const_spec(a.shape) for a in ins[1:]],
        out_specs=[_row_spec(tm, o.shape[1]) for o in outs],
        out_shape=outs,
        compiler_params=_cparams("parallel"),
        name="dsa_proj",
    )(*ins)


def _order_key(x):
    k = pltpu.bitcast(x, I32)
    k = k ^ ((k >> 31) & 0x7FFFFFFF)
    return jnp.where(x == 0.0, 0, k)


def _count(mask):
    return jnp.sum(jnp.where(mask, 1.0, 0.0), axis=-1, keepdims=True)


def _topk_mask(score, topk):
    key = _order_key(score)
    int_min = jnp.iinfo(jnp.int32).min
    kf = float(topk)

    def value_step(i, t):
        cand = t + lax.shift_left(jnp.int32(1), 31 - i)
        return jnp.where(_count(key >= cand) >= kf, cand, t)

    t = lax.fori_loop(0, 32, value_step, jnp.full((score.shape[0], 1), int_min, I32))
    above = key > t
    tied = key == t
    need = kf - _count(above)
    extra = _count(tied) - need
    col = lax.broadcasted_iota(I32, score.shape, 1)
    nbits = max(1, (score.shape[1] - 1).bit_length())

    def with_ties():
        def index_step(i, j):
            cand = j + lax.shift_left(jnp.int32(1), nbits - 1 - i)
            return jnp.where(_count(tied & (col < cand)) < need, cand, j)

        j = lax.fori_loop(0, nbits, index_step, jnp.zeros((score.shape[0], 1), I32))
        return jnp.where(above | (tied & (col <= j)), 1, 0)

    return lax.cond(jnp.max(extra) > 0.0, with_ties, lambda: jnp.where(above | tied, 1, 0)) > 0


def _dsa_attn_kernel(qidx_ref, misc_ref, qlat_ref, kidx_t_ref, ckv_t_ref, ckv_ref, o_ref, *, topk, scale):
    tq = qidx_ref.shape[0]
    seq = ckv_ref.shape[0]
    kr = DSA_KV_RANK
    w = misc_ref[...]
    kidx_t = kidx_t_ref[...]
    score = jnp.zeros((tq, seq), F32)
    for h in range(IDX_HEADS):
        rel = jnp.maximum(_dot(qidx_ref[:, h * LANES:(h + 1) * LANES], kidx_t), 0.0)
        score = score + rel * w[:, IDX_DIM + h:IDX_DIM + h + 1]
    q_pos = pl.program_id(1) * tq + lax.broadcasted_iota(I32, (tq, seq), 0)
    causal = lax.broadcasted_iota(I32, (tq, seq), 1) <= q_pos
    score = jnp.where(causal, score, NEG_BIG)
    sel = _topk_mask(score, topk) & causal
    ckv_t = ckv_t_ref[...]
    ckv = ckv_ref[...]
    for h in range(DSA_HEADS):
        logits = _dot(qlat_ref[:, h * kr:(h + 1) * kr], ckv_t) * scale
        logits = jnp.where(sel, logits, NEG_BIG)
        e = jnp.exp(logits - jnp.max(logits, axis=-1, keepdims=True))
        den = jnp.sum(e, axis=-1, keepdims=True)
        o_ref[:, h * kr:(h + 1) * kr] = (_dot(e.astype(BF16), ckv) / den).astype(o_ref.dtype)


def _dsa_attn(q_idx, misc, q_lat, kidx_t, ckv_t, ckv, batch, seq):
    n = q_idx.shape[0]
    tq = _tile(seq, DSA_Q_TILE)
    nq = seq // tq
    topk = min(IDX_TOPK_MAX, seq // 4)
    row = lambda wd: pl.BlockSpec((tq, wd), lambda b, i: (b * nq + i, 0))
    return pl.pallas_call(
        functools.partial(_dsa_attn_kernel, topk=topk, scale=DSA_HEAD_DIM ** -0.5),
        grid=(batch, nq),
        in_specs=[row(q_idx.shape[1]), row(LANES), row(q_lat.shape[1]),
                  pl.BlockSpec((None, LANES, seq), lambda b, i: (b, 0, 0)),
                  pl.BlockSpec((None, DSA_KV_RANK, seq), lambda b, i: (b, 0, 0)),
                  pl.BlockSpec((seq, DSA_KV_RANK), lambda b, i: (b, 0))],
        out_specs=row(q_lat.shape[1]),
        out_shape=jax.ShapeDtypeStruct((n, q_lat.shape[1]), BF16),
        compiler_params=_cparams("parallel", "parallel"),
        name="dsa_attn",
    )(q_idx, misc, q_lat, kidx_t, ckv_t, ckv)


def _dsa_out_ln_kernel(olat_ref, wuv_ref, wout_ref, x_ref, g_ref, b_ref, o_ref, *, alpha):
    kr, hd = DSA_KV_RANK, DSA_HEAD_DIM
    y = None
    for h in range(DSA_HEADS):
        oh = _dot(olat_ref[:, h * kr:(h + 1) * kr], wuv_ref[h]).astype(BF16)
        yh = _dot(oh, wout_ref[h * hd:(h + 1) * hd, :])
        y = yh if y is None else y + yh
    o_ref[...] = _layer_norm(alpha * x_ref[...] + y, g_ref[...], b_ref[...])


def _dsa_out_ln(o_lat, w_uv, w_out, x, g, b, alpha):
    n, d = x.shape
    tm = _tile(n, ROW_TILE)
    return pl.pallas_call(
        functools.partial(_dsa_out_ln_kernel, alpha=alpha),
        grid=(n // tm,),
        in_specs=[_row_spec(tm, o_lat.shape[1]), _const_spec(w_uv.shape), _const_spec(w_out.shape),
                  _row_spec(tm, d), _const_spec((1, d)), _const_spec((1, d))],
        out_specs=_row_spec(tm, d),
        out_shape=jax.ShapeDtypeStruct((n, d), F32),
        compiler_params=_cparams("parallel"),
        name="dsa_out_ln",
    )(o_lat, w_uv, w_out, x, g, b)


def _dsa_layer(x, batch, seq, w_in, q_g, kv_g, w_uq, w_uk, w_uv, w_qidx, ki_g, ki_b, w_out, g, b, alpha):
    d = x.shape[1]
    qk = DSA_Q_RANK + DSA_KV_RANK
    tail = w_in.shape[1] - qk
    w_in_pad = jnp.pad(w_in, ((0, 0), (0, LANES - tail))).astype(BF16)
    w_qidx_pad = jnp.pad(w_qidx.reshape(DSA_Q_RANK, IDX_HEADS, IDX_DIM),
                         ((0, 0), (0, 0), (0, LANES - IDX_DIM))).reshape(DSA_Q_RANK, IDX_HEADS * LANES)
    pad_lane = lambda v: jnp.pad(v, (0, LANES - v.shape[0])).reshape(1, LANES)
    q_lat, q_idx, ckv, misc = _dsa_proj(
        x, w_in_pad, q_g.reshape(1, -1), kv_g.reshape(1, -1), pad_lane(ki_g), pad_lane(ki_b),
        w_uq.astype(BF16), w_uk.astype(BF16), w_qidx_pad.astype(BF16))
    lane = jnp.arange(LANES)
    kidx = jnp.where(lane < IDX_DIM, misc, 0.0).astype(BF16)
    kidx_t = kidx.reshape(batch, seq, LANES).transpose(0, 2, 1)
    ckv_t = ckv.reshape(batch, seq, DSA_KV_RANK).transpose(0, 2, 1)
    o_lat = _dsa_attn(q_idx, misc, q_lat, kidx_t, ckv_t, ckv, batch, seq)
    return _dsa_out_ln(o_lat, w_uv.astype(BF16), w_out.astype(BF16), x, g, b, alpha)


def kernel(x, ln_g, ln_b, hg_w_in, hg_lower_bounds, hg_norm_g, hg_w_out, dsa_w_in, dsa_q_norm_g, dsa_kv_norm_g, dsa_w_uq, dsa_w_uk, dsa_w_uv, dsa_w_qidx, dsa_kidx_norm_g, dsa_kidx_norm_b, dsa_w_out, ffn_w_gate_up, ffn_w_down, moe_w_router, moe_w_gate_up, moe_w_down):
    batch, seq, d = x.shape
    depth = ln_g.shape[0]
    alpha = (2 * depth) ** 0.25
    lb_all = jax.nn.softmax(hg_lower_bounds.astype(F32), axis=0)
    lb_all = jnp.cumsum(lb_all, axis=0) - lb_all[0]
    h = x.reshape(batch * seq, d)
    row = lambda v: v.reshape(1, d)
    for layer in range(depth):
        j = layer // 2
        g0, b0, g1, b1 = row(ln_g[layer, 0]), row(ln_b[layer, 0]), row(ln_g[layer, 1]), row(ln_b[layer, 1])
        if layer % 2 == 0:
            q, k, lf, v, g = _hgrn_proj(h, hg_w_in[j].astype(BF16), row(lb_all[j]))
            o = _hgrn_rec(q, k, lf, v, g, hg_norm_g[j], batch, seq)
            h = _proj_ln(o, hg_w_out[j].astype(BF16), h, g0, b0, alpha)
            h = _ffn_ln(h, ffn_w_gate_up[j].astype(BF16), ffn_w_down[j].astype(BF16), g1, b1, alpha)
        else:
            h = _dsa_layer(h, batch, seq, dsa_w_in[j], dsa_q_norm_g[j], dsa_kv_norm_g[j], dsa_w_uq[j],
                           dsa_w_uk[j], dsa_w_uv[j], dsa_w_qidx[j], dsa_kidx_norm_g[j], dsa_kidx_norm_b[j],
                           dsa_w_out[j], g0, b0, alpha)
            w_router_pad = jnp.pad(moe_w_router[j], ((0, 0), (0, LANES - N_EXPERTS)))
            h = _moe_layer(h, w_router_pad, moe_w_gate_up[j].astype(BF16), moe_w_down[j].astype(BF16),
                           g1, b1, alpha)
    return h.reshape(batch, seq, d)
```

```python
import functools

import numpy as np
import jax
import jax.numpy as jnp
from jax import lax
from jax.experimental import pallas as pl
from jax.experimental.pallas import tpu as pltpu

F32, BF16, I32 = jnp.float32, jnp.bfloat16, jnp.int32

HG_HEADS = 8
DSA_HEADS = 8
DSA_HEAD_DIM = 128
DSA_Q_RANK = 256
DSA_KV_RANK = 256
IDX_HEADS = 8
IDX_DIM = 64
IDX_TOPK_MAX = 256
N_EXPERTS = 8
LN_EPS = 1e-5
RMS_EPS = 1e-6
NEG_BIG = -1e30

LANES = 128
V7X_VMEM_BYTES = 64 * 1024 * 1024
VMEM_LIMIT = V7X_VMEM_BYTES * 7 // 8

ROW_TILE = 512
FFN_ROW_TILE = 1024
FFN_F_TILE = 512
MOE_ROW_TILE = 512
HG_TIME_TILE = 512
HG_CHUNK = 128
DSA_Q_TILE = 256
DSA_KEY_TILE = 512
DSA_SEARCH_ROWS = 128
DSA_SOFTMAX_ROWS = 128
DSA_GROUP_ROWS = 512
LOG2_E = 1.4426950408889634
GATHER_TILE = 512
COMBINE_TILE = 256


def _cparams(*sem):
    return pltpu.CompilerParams(dimension_semantics=sem, vmem_limit_bytes=VMEM_LIMIT)


def _tile(n, t):
    t = min(n, t)
    assert n % t == 0, (n, t)
    return t


def _dot(a, b):
    return jnp.dot(a, b, preferred_element_type=F32)


def _dot_nt(a, b):
    return lax.dot_general(a, b, (((1,), (1,)), ((), ())), preferred_element_type=F32)


def _dot_tn(a, b):
    return lax.dot_general(a, b, (((0,), (0,)), ((), ())), preferred_element_type=F32)


def _layer_norm(y, g, b):
    mu = jnp.mean(y, axis=-1, keepdims=True)
    yc = y - mu
    var = jnp.mean(yc * yc, axis=-1, keepdims=True)
    return yc * lax.rsqrt(var + LN_EPS) * g + b


def _row_spec(t, d):
    return pl.BlockSpec((t, d), lambda i: (i, 0))


def _const_spec(shape):
    return pl.BlockSpec(shape, lambda *_: (0,) * len(shape))


def _hgrn_proj_kernel(x_ref, w_ref, lb_ref, q_ref, k_ref, lf_ref, v_ref, g_ref):
    d = x_ref.shape[1]
    xb = x_ref[...].astype(BF16)
    lb = lb_ref[...]
    q_ref[...] = _dot(xb, w_ref[:, 0:d])
    fp = _dot(xb, w_ref[:, d:2 * d])
    lf_ref[...] = jnp.log(lb + (1.0 - lb) * jax.nn.sigmoid(fp))
    k_ref[...] = (1.0 - lb) * jax.nn.sigmoid(-fp)
    v_ref[...] = _dot(xb, w_ref[:, 2 * d:3 * d]).astype(BF16)
    g_ref[...] = _dot(xb, w_ref[:, 3 * d:4 * d])


def _hgrn_proj(x, w_in, lb):
    n, d = x.shape
    tm = _tile(n, ROW_TILE)
    outs = [jax.ShapeDtypeStruct((n, d), dt) for dt in (F32, F32, F32, BF16, F32)]
    return pl.pallas_call(
        _hgrn_proj_kernel,
        grid=(n // tm,),
        in_specs=[_row_spec(tm, d), _const_spec((d, 4 * d)), _const_spec((1, d))],
        out_specs=[_row_spec(tm, d)] * 5,
        out_shape=outs,
        compiler_params=_cparams("parallel"),
        name="hgrn_proj",
    )(x, w_in, lb)


def _cumsum_rows(x):
    c = x.shape[0]
    row = lax.broadcasted_iota(I32, x.shape, 0)
    sh = 1
    while sh < min(c, 8):
        x = x + jnp.where(row >= sh, pltpu.roll(x, sh, axis=0), 0.0)
        sh *= 2
    while sh < c:
        x = x + jnp.concatenate([jnp.zeros((sh, x.shape[1]), x.dtype), x[:c - sh, :]], axis=0)
        sh *= 2
    return x


def _level_decay(b, lf, m):
    c = b.shape[0]
    if m >= 4:
        parts = []
        for lo in range(0, c, 2 * m):
            parts.append(b[lo:lo + 2 * m, :] - b[lo + m - 1:lo + m, :])
        d = parts[0] if len(parts) == 1 else jnp.concatenate(parts, axis=0)
    else:
        row = lax.broadcasted_iota(I32, b.shape, 0)
        if m == 1:
            d = jnp.where((row & 1) == 0, 0.0, lf)
        else:
            ph = row & 3
            nxt = pltpu.roll(lf, c - 1, axis=0)
            prv = pltpu.roll(lf, 1, axis=0)
            d = jnp.where(ph == 0, nxt, jnp.where(ph == 1, 0.0, jnp.where(ph == 2, lf, lf + prv)))
    return jnp.exp(-jnp.abs(d))


def _hgrn_levels(chunk):
    levels = []
    m = chunk // 2
    while m >= 1:
        levels.append(m)
        m //= 2
    return levels


def _hgrn_masks(chunk):
    t = np.arange(chunk)[:, None]
    s = np.arange(chunk)[None, :]
    x = t ^ s
    masks = [((t > s) & (x >= m) & (x < 2 * m)) for m in _hgrn_levels(chunk)]
    masks.append(t == s)
    return jnp.asarray(np.stack(masks).astype(np.float32))


def _hgrn_rec_kernel(q_ref, k_ref, lf_ref, v_ref, g_ref, ng_ref, masks_ref, o_ref, st_ref, *, chunk):
    h = pl.program_id(1)
    levels = _hgrn_levels(chunk)

    @pl.when(pl.program_id(2) == 0)
    def _():
        st_ref[...] = jnp.zeros_like(st_ref)

    ng = ng_ref[pl.ds(h, 1), :]

    def body(c, carry):
        sl = pl.ds(pl.multiple_of(c * chunk, chunk), chunk)
        q, k, lf, v, g = q_ref[sl, :], k_ref[sl, :], lf_ref[sl, :], v_ref[sl, :], g_ref[sl, :]
        b = _cumsum_rows(lf)
        s = masks_ref[len(levels)] * _dot_nt(q.astype(BF16), k.astype(BF16))
        for li, m in enumerate(levels):
            e = _level_decay(b, lf, m)
            s = s + masks_ref[li] * _dot_nt((q * e).astype(BF16), (k * e).astype(BF16))
        o = _dot(s.astype(BF16), v)
        st = st_ref[...]
        b_last = b[chunk - 1:chunk, :]
        o = o + _dot_nt((q * jnp.exp(b)).astype(BF16), st.astype(BF16))
        kt = (k * jnp.exp(b_last - b)).astype(BF16)
        st_ref[...] = st * jnp.exp(b_last) + _dot_tn(v, kt)
        ms = jnp.mean(o * o, axis=-1, keepdims=True)
        o = o * lax.rsqrt(ms + RMS_EPS) * ng
        o_ref[sl, :] = (o * (g * jax.nn.sigmoid(g))).astype(o_ref.dtype)
        return carry

    lax.fori_loop(0, q_ref.shape[0] // chunk, body, 0)


def _hgrn_rec(q, k, lf, v, g, norm_g, batch, seq):
    n, d = q.shape
    dh = d // HG_HEADS
    tt = _tile(seq, HG_TIME_TILE)
    chunk = _tile(tt, HG_CHUNK)
    nt = seq // tt
    masks = _hgrn_masks(chunk)
    blk = pl.BlockSpec((tt, dh), lambda b, h, t: (b * nt + t, h))
    return pl.pallas_call(
        functools.partial(_hgrn_rec_kernel, chunk=chunk),
        grid=(batch, HG_HEADS, nt),
        in_specs=[blk, blk, blk, blk, blk, _const_spec(norm_g.shape), _const_spec(masks.shape)],
        out_specs=blk,
        out_shape=jax.ShapeDtypeStruct((n, d), BF16),
        scratch_shapes=[pltpu.VMEM((dh, dh), F32)],
        compiler_params=_cparams("parallel", "parallel", "arbitrary"),
        name="hgrn_rec",
    )(q, k, lf, v, g, norm_g, masks)


def _proj_ln_kernel(a_ref, w_ref, x_ref, g_ref, b_ref, o_ref, *, alpha):
    y = _dot(a_ref[...], w_ref[...])
    o_ref[...] = _layer_norm(alpha * x_ref[...] + y, g_ref[...], b_ref[...])


def _proj_ln(a, w, x, g, b, alpha):
    n, d = x.shape
    ka = a.shape[1]
    tm = _tile(n, ROW_TILE)
    return pl.pallas_call(
        functools.partial(_proj_ln_kernel, alpha=alpha),
        grid=(n // tm,),
        in_specs=[_row_spec(tm, ka), _const_spec((ka, d)), _row_spec(tm, d), _const_spec((1, d)),
                  _const_spec((1, d))],
        out_specs=_row_spec(tm, d),
        out_shape=jax.ShapeDtypeStruct((n, d), F32),
        compiler_params=_cparams("parallel"),
        name="proj_ln",
    )(a, w, x, g, b)


def _swiglu_step(xb, wg_ref, wu_ref, wd_ref):
    gate = _dot(xb, wg_ref[...])
    up = _dot(xb, wu_ref[...])
    act = (gate * jax.nn.sigmoid(gate) * up).astype(BF16)
    return _dot(act, wd_ref[...])


def _ffn_ln_kernel(x_ref, wg_ref, wu_ref, wd_ref, g_ref, b_ref, o_ref, acc_ref, *, alpha):
    f = pl.program_id(1)
    y = _swiglu_step(x_ref[...].astype(BF16), wg_ref, wu_ref, wd_ref)

    @pl.when(f == 0)
    def _():
        acc_ref[...] = y

    @pl.when(f > 0)
    def _():
        acc_ref[...] += y

    @pl.when(f == pl.num_programs(1) - 1)
    def _():
        o_ref[...] = _layer_norm(alpha * x_ref[...] + acc_ref[...], g_ref[...], b_ref[...])


def _ffn_ln(x, w_gate_up, w_down, g, b, alpha):
    n, d = x.shape
    ff = w_down.shape[0]
    tm = _tile(n, FFN_ROW_TILE)
    tf = _tile(ff, FFN_F_TILE)
    nf = ff // tf
    return pl.pallas_call(
        functools.partial(_ffn_ln_kernel, alpha=alpha),
        grid=(n // tm, nf),
        in_specs=[pl.BlockSpec((tm, d), lambda i, f: (i, 0)),
                  pl.BlockSpec((d, tf), lambda i, f: (0, f)),
                  pl.BlockSpec((d, tf), lambda i, f: (0, f + nf)),
                  pl.BlockSpec((tf, d), lambda i, f: (f, 0)),
                  _const_spec((1, d)), _const_spec((1, d))],
        out_specs=pl.BlockSpec((tm, d), lambda i, f: (i, 0)),
        out_shape=jax.ShapeDtypeStruct((n, d), F32),
        scratch_shapes=[pltpu.VMEM((tm, d), F32)],
        compiler_params=_cparams("parallel", "arbitrary"),
        name="ffn_ln",
    )(x, w_gate_up, w_gate_up, w_down, g, b)


def _rows_from_tiles(t_ref):
    return jnp.concatenate([t_ref[:, c, :] for c in range(t_ref.shape[1])], axis=1)


def _rows_to_tiles(t_ref, y):
    for c in range(t_ref.shape[1]):
        t_ref[:, c, :] = y[:, c * LANES:(c + 1) * LANES]


def _moe_ffn_kernel(te_ref, nt_ref, x_ref, wg_ref, wu_ref, wd_ref, o_ref, xb_ref, acc_ref):
    i, f = pl.program_id(0), pl.program_id(1)
    live = i < nt_ref[0]
    last = f == pl.num_programs(1) - 1

    @pl.when(live)
    def _():
        @pl.when(f == 0)
        def _():
            xb_ref[...] = _rows_from_tiles(x_ref).astype(BF16)

        y = _swiglu_step(xb_ref[...], wg_ref, wu_ref, wd_ref)

        @pl.when(f == 0)
        def _():
            acc_ref[...] = y

        @pl.when(f > 0)
        def _():
            acc_ref[...] += y

        @pl.when(last)
        def _():
            _rows_to_tiles(o_ref, acc_ref[...])

    @pl.when(jnp.logical_not(live) & last)
    def _():
        o_ref[...] = jnp.zeros_like(o_ref)


def _moe_ffn(tile_expert, n_live, xs, w_gate_up, w_down, tm):
    p, sub, lanes = xs.shape
    d = sub * lanes
    ff = w_down.shape[1]
    assert p % tm == 0
    tf = _tile(ff, FFN_F_TILE)
    nf = ff // tf
    tile_blk = pl.BlockSpec((tm, sub, lanes), lambda i, f, te, nt: (i, 0, 0))
    grid_spec = pltpu.PrefetchScalarGridSpec(
        num_scalar_prefetch=2,
        grid=(p // tm, nf),
        in_specs=[tile_blk,
                  pl.BlockSpec((None, d, tf), lambda i, f, te, nt: (te[i], 0, f)),
                  pl.BlockSpec((None, d, tf), lambda i, f, te, nt: (te[i], 0, f + nf)),
                  pl.BlockSpec((None, tf, d), lambda i, f, te, nt: (te[i], f, 0))],
        out_specs=tile_blk,
        scratch_shapes=[pltpu.VMEM((tm, d), BF16), pltpu.VMEM((tm, d), F32)],
    )
    return pl.pallas_call(
        _moe_ffn_kernel,
        grid_spec=grid_spec,
        out_shape=jax.ShapeDtypeStruct(xs.shape, F32),
        compiler_params=_cparams("parallel", "arbitrary"),
        name="moe_ffn",
    )(tile_expert, n_live, xs, w_gate_up, w_gate_up, w_down)


def _router_kernel(x_ref, w_ref, tril_ref, gate_ref, idx_ref, cnt_ref, carry_ref):
    @pl.when(pl.program_id(0) == 0)
    def _():
        carry_ref[...] = jnp.zeros_like(carry_ref)

    logits = jnp.dot(x_ref[...], w_ref[...], preferred_element_type=F32, precision=lax.Precision.HIGHEST)
    lane = lax.broadcasted_iota(I32, logits.shape, 1)
    l1 = jnp.where(lane < N_EXPERTS, logits, -jnp.inf)
    m1 = jnp.max(l1, axis=-1, keepdims=True)
    i1 = jnp.min(jnp.where(l1 == m1, lane, LANES), axis=-1, keepdims=True)
    l2 = jnp.where(lane == i1, -jnp.inf, l1)
    m2 = jnp.max(l2, axis=-1, keepdims=True)
    i2 = jnp.min(jnp.where(l2 == m2, lane, LANES), axis=-1, keepdims=True)
    e = jnp.exp(m2 - m1)
    den = 1.0 + e
    gate_ref[...] = jnp.where(lane == 0, 1.0 / den, jnp.where(lane == 1, e / den, 0.0))
    oh1 = jnp.where(lane == i1, 1.0, 0.0)
    oh2 = jnp.where(lane == i2, 1.0, 0.0)
    oh = oh1 + oh2
    before = _dot(tril_ref[...], oh.astype(BF16)) + carry_ref[...]
    r1 = jnp.sum(oh1 * before, axis=-1, keepdims=True).astype(I32)
    r2 = jnp.sum(oh2 * before, axis=-1, keepdims=True).astype(I32)
    carry_ref[...] += jnp.sum(oh, axis=0, keepdims=True)
    cnt_ref[...] = carry_ref[...]
    idx_ref[...] = jnp.where(lane == 0, i1, jnp.where(lane == 1, i2, jnp.where(lane == 2, r1, jnp.where(
        lane == 3, r2, 0))))


def _router(x, w_router_pad):
    n, d = x.shape
    tm = _tile(n, ROW_TILE)
    tril = jnp.asarray(np.tril(np.ones((tm, tm), np.float32), -1), BF16)
    return pl.pallas_call(
        _router_kernel,
        grid=(n // tm,),
        in_specs=[_row_spec(tm, d), _const_spec((d, LANES)), _const_spec((tm, tm))],
        out_specs=[_row_spec(tm, LANES), _row_spec(tm, LANES), _const_spec((1, LANES))],
        out_shape=[jax.ShapeDtypeStruct((n, LANES), F32), jax.ShapeDtypeStruct((n, LANES), I32),
                   jax.ShapeDtypeStruct((1, LANES), F32)],
        scratch_shapes=[pltpu.VMEM((1, LANES), F32)],
        compiler_params=_cparams("arbitrary"),
        name="router",
    )(x, w_router_pad, tril)


def _moe_dispatch_kernel(p1_ref, p2_ref, x_ref, xs_init_hbm, xs_hbm, stage_ref, sem):
    del xs_init_hbm
    td = x_ref.shape[0]
    base = pl.program_id(0) * td
    _rows_to_tiles(stage_ref, x_ref[...])

    def copies(r, row1, row2):
        return (pltpu.make_async_copy(stage_ref.at[r], xs_hbm.at[row1], sem),
                pltpu.make_async_copy(stage_ref.at[r], xs_hbm.at[row2], sem))

    def issue(r, c):
        for cp in copies(r, p1_ref[base + r], p2_ref[base + r]):
            cp.start()
        return c

    def drain(r, c):
        for cp in copies(r, 0, 0):
            cp.wait()
        return c

    lax.fori_loop(0, td, issue, 0)
    lax.fori_loop(0, td, drain, 0)


def _moe_dispatch(p1, p2, x, p_rows):
    n, d = x.shape
    sub = d // LANES
    td = _tile(n, GATHER_TILE)
    xs_init = jnp.zeros((p_rows, sub, LANES), F32)
    grid_spec = pltpu.PrefetchScalarGridSpec(
        num_scalar_prefetch=2,
        grid=(n // td,),
        in_specs=[pl.BlockSpec((td, d), lambda i, a, c: (i, 0)), pl.BlockSpec(memory_space=pl.ANY)],
        out_specs=pl.BlockSpec(memory_space=pl.ANY),
        scratch_shapes=[pltpu.VMEM((td, sub, LANES), F32), pltpu.SemaphoreType.DMA(())],
    )
    return pl.pallas_call(
        _moe_dispatch_kernel,
        grid_spec=grid_spec,
        out_shape=jax.ShapeDtypeStruct(xs_init.shape, F32),
        input_output_aliases={3: 0},
        compiler_params=_cparams("arbitrary"),
        name="moe_dispatch",
    )(p1, p2, x, xs_init)


def _moe_combine_ln_kernel(p1_ref, p2_ref, x_ref, gate_ref, ys_hbm, g_ref, b_ref, o_ref, buf_ref, sem, *, alpha):
    i, nsteps = pl.program_id(0), pl.num_programs(0)
    tc = o_ref.shape[0]

    def copies(slot, r, row1, row2):
        return (pltpu.make_async_copy(ys_hbm.at[row1], buf_ref.at[slot, 0, r], sem.at[slot]),
                pltpu.make_async_copy(ys_hbm.at[row2], buf_ref.at[slot, 1, r], sem.at[slot]))

    def issue(step, slot):
        def body(r, c):
            for cp in copies(slot, r, p1_ref[step * tc + r], p2_ref[step * tc + r]):
                cp.start()
            return c

        lax.fori_loop(0, tc, body, 0)

    @pl.when(i == 0)
    def _():
        issue(0, 0)

    @pl.when(i + 1 < nsteps)
    def _():
        issue(i + 1, (i + 1) % 2)

    slot = i % 2

    def drain(r, c):
        for cp in copies(slot, r, 0, 0):
            cp.wait()
        return c

    lax.fori_loop(0, tc, drain, 0)
    w = gate_ref[...]
    y = w[:, 0:1] * _rows_from_tiles(buf_ref.at[slot, 0]) + w[:, 1:2] * _rows_from_tiles(buf_ref.at[slot, 1])
    o_ref[...] = _layer_norm(alpha * x_ref[...] + y, g_ref[...], b_ref[...])


def _moe_combine_ln(p1, p2, x, gate, ys, g, b, alpha):
    n, d = x.shape
    sub = d // LANES
    tc = _tile(n, COMBINE_TILE)
    grid_spec = pltpu.PrefetchScalarGridSpec(
        num_scalar_prefetch=2,
        grid=(n // tc,),
        in_specs=[pl.BlockSpec((tc, d), lambda i, a, c: (i, 0)),
                  pl.BlockSpec((tc, LANES), lambda i, a, c: (i, 0)),
                  pl.BlockSpec(memory_space=pl.ANY),
                  pl.BlockSpec((1, d), lambda i, a, c: (0, 0)),
                  pl.BlockSpec((1, d), lambda i, a, c: (0, 0))],
        out_specs=pl.BlockSpec((tc, d), lambda i, a, c: (i, 0)),
        scratch_shapes=[pltpu.VMEM((2, 2, tc, sub, LANES), F32), pltpu.SemaphoreType.DMA((2,))],
    )
    return pl.pallas_call(
        functools.partial(_moe_combine_ln_kernel, alpha=alpha),
        grid_spec=grid_spec,
        out_shape=jax.ShapeDtypeStruct((n, d), F32),
        compiler_params=_cparams("arbitrary"),
        name="moe_combine_ln",
    )(p1, p2, x, gate, ys, g, b)


def _moe_layer(x, w_router_pad, w_gate_up, w_down, g, b, alpha):
    n = x.shape[0]
    gate, idx, cnt = _router(x, w_router_pad)
    tm = min(MOE_ROW_TILE, 2 * n)
    p_rows = 2 * n + N_EXPERTS * tm
    counts = cnt[0, :N_EXPERTS].astype(I32)
    padded = ((counts + tm - 1) // tm) * tm
    pends = jnp.cumsum(padded)
    pstarts = pends - padded
    p1 = pstarts[idx[:, 0]] + idx[:, 2]
    p2 = pstarts[idx[:, 1]] + idx[:, 3]
    tile_start = jnp.arange(p_rows // tm, dtype=I32) * tm
    tile_expert = jnp.minimum(jnp.sum(tile_start[:, None] >= pends[None, :], axis=1), N_EXPERTS - 1).astype(I32)
    n_live = (pends[-1] // tm).astype(I32).reshape(1)
    xs = _moe_dispatch(p1, p2, x, p_rows)
    ys = _moe_ffn(tile_expert, n_live, xs, w_gate_up, w_down, tm)
    return _moe_combine_ln(p1, p2, x, gate, ys, g, b, alpha)


def _rms_norm(x, g):
    ms = jnp.mean(x * x, axis=-1, keepdims=True)
    return x * lax.rsqrt(ms + RMS_EPS) * g


def _dsa_proj_kernel(x_ref, win_ref, qg_ref, kvg_ref, kig_ref, kib_ref, wuq_ref, wuk_ref, wqi_ref,
                     qlat_ref, qidx_ref, ckv_ref, misc_ref, *, w_scale):
    xb = x_ref[...].astype(BF16)
    qr, kr = DSA_Q_RANK, DSA_KV_RANK
    cq = _rms_norm(_dot(xb, win_ref[:, 0:qr]), qg_ref[...]).astype(BF16)
    ckv_ref[...] = _rms_norm(_dot(xb, win_ref[:, qr:qr + kr]), kvg_ref[...]).astype(ckv_ref.dtype)
    misc = _dot(xb, win_ref[:, qr + kr:qr + kr + LANES])
    lane = lax.broadcasted_iota(I32, misc.shape, 1)
    is_key = lane < IDX_DIM
    mu = jnp.sum(jnp.where(is_key, misc, 0.0), axis=-1, keepdims=True) * (1.0 / IDX_DIM)
    kc = jnp.where(is_key, misc - mu, 0.0)
    var = jnp.sum(kc * kc, axis=-1, keepdims=True) * (1.0 / IDX_DIM)
    kn = kc * lax.rsqrt(var + LN_EPS) * kig_ref[...] + kib_ref[...]
    misc_ref[...] = jnp.where(is_key, kn, misc * w_scale)
    q = _dot(cq, wuq_ref[...])
    hd = DSA_HEAD_DIM
    for h in range(DSA_HEADS):
        qidx_ref[h] = _dot(cq, wqi_ref[:, h * LANES:(h + 1) * LANES]).astype(qidx_ref.dtype)
        qlat_ref[h] = _dot(q[:, h * hd:(h + 1) * hd].astype(BF16), wuk_ref[h]).astype(qlat_ref.dtype)


def _head_rows_spec(t, wd):
    return pl.BlockSpec((DSA_HEADS, t, wd), lambda i: (0, i, 0))


def _dsa_proj(x, w_in_pad, q_g, kv_g, ki_g, ki_b, w_uq, w_uk, w_qidx_pad):
    n, d = x.shape
    tm = _tile(n, ROW_TILE)
    w_scale = IDX_HEADS ** -0.5 * IDX_DIM ** -0.5
    outs = [jax.ShapeDtypeStruct((DSA_HEADS, n, DSA_KV_RANK), BF16),
            jax.ShapeDtypeStruct((IDX_HEADS, n, LANES), BF16),
            jax.ShapeDtypeStruct((n, DSA_KV_RANK), BF16),
            jax.ShapeDtypeStruct((n, LANES), F32)]
    ins = [x, w_in_pad, q_g, kv_g, ki_g, ki_b, w_uq, w_uk, w_qidx_pad]
    return pl.pallas_call(
        functools.partial(_dsa_proj_kernel, w_scale=w_scale),
        grid=(n // tm,),
        in_specs=[_row_spec(tm, d)] + [_const_spec(a.shape) for a in ins[1:]],
        out_specs=[_head_rows_spec(tm, DSA_KV_RANK), _head_rows_spec(tm, LANES), _row_spec(tm, DSA_KV_RANK),
                   _row_spec(tm, LANES)],
        out_shape=outs,
        compiler_params=_cparams("parallel"),
        name="dsa_proj",
    )(*ins)


def _order_key(x):
    k = pltpu.bitcast(x, I32)
    k = k ^ ((k >> 31) & 0x7FFFFFFF)
    return jnp.where(x == 0.0, 0, k)


def _dsa_attn_kernel(qidx_ref, misc_ref, qlat_ref, kidx_t_ref, ckv_t_ref, ckv_ref, o_ref,
                     key_ref, thr_ref, bias_ref, s_ref, p_ref, m_ref, l_ref, alpha_ref, acc_ref,
                     *, topk, scale, tk, sr, rc, rg):
    nh, tq, kr = qlat_ref.shape
    seq = ckv_ref.shape[0]
    q0 = pl.program_id(1) * tq
    n_tiles = (q0 + tq + tk - 1) // tk
    kf = float(topk)
    int_min = jnp.iinfo(jnp.int32).min
    nbits = max(1, (seq - 1).bit_length())
    w = misc_ref[...]
    q_pos = q0 + lax.broadcasted_iota(I32, (tq, tk), 0)
    col0 = lax.broadcasted_iota(I32, (tq, tk), 1)
    lane = lax.broadcasted_iota(I32, (sr, LANES), 1)

    def key_tile(kb):
        return pl.ds(pl.multiple_of(kb * tk, tk), tk)

    q_idx_all = qidx_ref[...].reshape(nh * tq, LANES)

    def score_tile(kb, c):
        r = _dot(q_idx_all, kidx_t_ref[:, key_tile(kb)]).reshape(nh, tq, tk)
        score = jnp.zeros((tq, tk), F32)
        for h in range(nh):
            score = score + jnp.maximum(r[h], 0.0) * w[:, IDX_DIM + h:IDX_DIM + h + 1]
        score = jnp.where(kb * tk + col0 <= q_pos, score, NEG_BIG)
        key_ref[:, key_tile(kb)] = _order_key(score)
        return c

    lax.fori_loop(0, n_tiles, score_tile, 0)

    def count_rows(rows, pred):
        def tile(kb, acc):
            blk = key_ref[rows, key_tile(kb)]
            for g in range(tk // LANES):
                hit = pred(blk[:, g * LANES:(g + 1) * LANES], kb * tk + g * LANES)
                acc = acc + jnp.where(hit, 1.0, 0.0)
            return acc

        acc = lax.fori_loop(0, n_tiles, tile, jnp.zeros((sr, LANES), F32))
        return jnp.sum(acc, axis=-1, keepdims=True)

    for part in range(tq // sr):
        rows = pl.ds(part * sr, sr)

        def value_step(it, t):
            cand = t + lax.shift_left(jnp.int32(1), 31 - it)
            cand_b = jnp.broadcast_to(cand, (sr, LANES))
            return jnp.where(count_rows(rows, lambda k, c: k >= cand_b) >= kf, cand, t)

        t = lax.fori_loop(0, 32, value_step, jnp.full((sr, 1), int_min, I32))
        t_b = jnp.broadcast_to(t, (sr, LANES))
        need = kf - count_rows(rows, lambda k, c: k > t_b)
        extra = count_rows(rows, lambda k, c: k == t_b) - need

        def last_kept_tie():
            def index_step(it, j):
                cand = j + lax.shift_left(jnp.int32(1), nbits - 1 - it)
                cand_b = jnp.broadcast_to(cand, (sr, LANES))
                below = count_rows(rows, lambda k, c: (k == t_b) & (c + lane < cand_b))
                return jnp.where(below < need, cand, j)

            return lax.fori_loop(0, nbits, index_step, jnp.zeros((sr, 1), I32))

        j = lax.cond(jnp.max(extra) > 0.0, last_kept_tie, lambda: jnp.full((sr, 1), seq, I32))
        thr_ref[rows, :] = jnp.where(lane == 0, t, jnp.where(lane == 1, j, 0))

    thr = thr_ref[...]
    t_all, j_all = thr[:, 0:1], thr[:, 1:2]
    q_lat_all = qlat_ref[...].reshape(nh * tq, kr)
    m_ref[...] = jnp.full(m_ref.shape, NEG_BIG, F32)
    l_ref[...] = jnp.zeros(l_ref.shape, F32)
    acc_ref[...] = jnp.zeros(acc_ref.shape, F32)
    c2 = scale * LOG2_E

    def attn_tile(kb, c):
        key = key_ref[:, key_tile(kb)]
        col = kb * tk + col0
        sel = ((key > t_all) | ((key == t_all) & (col <= j_all))) & (col <= q_pos)
        bias_ref[...] = jnp.where(sel, 0.0, NEG_BIG)
        ckv_t = ckv_t_ref[:, key_tile(kb)]
        ckv = ckv_ref[key_tile(kb), :]
        for g0 in range(0, nh * tq, rg):
            grp = slice(g0, g0 + rg)
            s_ref[grp, :] = _dot(q_lat_all[grp, :], ckv_t)
            for r0 in range(g0, g0 + rg, rc):
                rows = slice(r0, r0 + rc)
                s = s_ref[rows, :] + bias_ref[r0 % tq:r0 % tq + rc, :]
                m_prev = m_ref[rows, :]
                m_new = jnp.maximum(m_prev, jnp.max(s, axis=-1, keepdims=True))
                alpha = jnp.exp2((m_prev - m_new) * c2)
                p = jnp.exp2((s - m_new) * c2)
                l_ref[rows, :] = alpha * l_ref[rows, :] + jnp.sum(p, axis=-1, keepdims=True)
                m_ref[rows, :] = m_new
                alpha_ref[rows, :] = alpha
                p_ref[rows, :] = p.astype(BF16)
            acc_ref[grp, :] = alpha_ref[grp, :] * acc_ref[grp, :] + _dot(p_ref[grp, :], ckv)
        return c

    lax.fori_loop(0, n_tiles, attn_tile, 0)
    o_ref[...] = (acc_ref[...] / l_ref[...]).reshape(nh, tq, kr).astype(o_ref.dtype)


def _dsa_attn(q_idx, misc, q_lat, kidx_t, ckv_t, ckv, batch, seq):
    nh, n, kr = q_lat.shape
    tq = _tile(seq, DSA_Q_TILE)
    tk = _tile(seq, DSA_KEY_TILE)
    sr = _tile(tq, DSA_SEARCH_ROWS)
    rc = _tile(tq, DSA_SOFTMAX_ROWS)
    nq = seq // tq
    topk = min(IDX_TOPK_MAX, seq // 4)
    heads = lambda wd: pl.BlockSpec((nh, tq, wd), lambda b, i: (0, b * nq + i, 0))
    stat = pltpu.VMEM((nh * tq, 1), F32)
    return pl.pallas_call(
        functools.partial(_dsa_attn_kernel, topk=topk, scale=DSA_HEAD_DIM ** -0.5, tk=tk, sr=sr, rc=rc,
                          rg=_tile(nh * tq, DSA_GROUP_ROWS)),
        grid=(batch, nq),
        in_specs=[heads(LANES), pl.BlockSpec((tq, LANES), lambda b, i: (b * nq + i, 0)), heads(kr),
                  pl.BlockSpec((None, LANES, seq), lambda b, i: (b, 0, 0)),
                  pl.BlockSpec((None, kr, seq), lambda b, i: (b, 0, 0)),
                  pl.BlockSpec((seq, kr), lambda b, i: (b, 0))],
        out_specs=heads(kr),
        out_shape=jax.ShapeDtypeStruct((nh, n, kr), BF16),
        scratch_shapes=[pltpu.VMEM((tq, seq), I32), pltpu.VMEM((tq, LANES), I32), pltpu.VMEM((tq, tk), F32),
                        pltpu.VMEM((nh * tq, tk), F32), pltpu.VMEM((nh * tq, tk), BF16),
                        stat, stat, stat, pltpu.VMEM((nh * tq, kr), F32)],
        compiler_params=_cparams("parallel", "parallel"),
        name="dsa_attn",
    )(q_idx, misc, q_lat, kidx_t, ckv_t, ckv)


def _dsa_out_ln_kernel(olat_ref, wuv_ref, wout_ref, x_ref, g_ref, b_ref, o_ref, *, alpha):
    hd = DSA_HEAD_DIM
    y = None
    for h in range(DSA_HEADS):
        oh = _dot(olat_ref[h], wuv_ref[h]).astype(BF16)
        yh = _dot(oh, wout_ref[h * hd:(h + 1) * hd, :])
        y = yh if y is None else y + yh
    o_ref[...] = _layer_norm(alpha * x_ref[...] + y, g_ref[...], b_ref[...])


def _dsa_out_ln(o_lat, w_uv, w_out, x, g, b, alpha):
    n, d = x.shape
    tm = _tile(n, ROW_TILE)
    return pl.pallas_call(
        functools.partial(_dsa_out_ln_kernel, alpha=alpha),
        grid=(n // tm,),
        in_specs=[_head_rows_spec(tm, o_lat.shape[2]), _const_spec(w_uv.shape), _const_spec(w_out.shape),
                  _row_spec(tm, d), _const_spec((1, d)), _const_spec((1, d))],
        out_specs=_row_spec(tm, d),
        out_shape=jax.ShapeDtypeStruct((n, d), F32),
        compiler_params=_cparams("parallel"),
        name="dsa_out_ln",
    )(o_lat, w_uv, w_out, x, g, b)


def _dsa_layer(x, batch, seq, w_in, q_g, kv_g, w_uq, w_uk, w_uv, w_qidx, ki_g, ki_b, w_out, g, b, alpha):
    d = x.shape[1]
    qk = DSA_Q_RANK + DSA_KV_RANK
    tail = w_in.shape[1] - qk
    w_in_pad = jnp.pad(w_in, ((0, 0), (0, LANES - tail))).astype(BF16)
    w_qidx_pad = jnp.pad(w_qidx.reshape(DSA_Q_RANK, IDX_HEADS, IDX_DIM),
                         ((0, 0), (0, 0), (0, LANES - IDX_DIM))).reshape(DSA_Q_RANK, IDX_HEADS * LANES)
    pad_lane = lambda v: jnp.pad(v, (0, LANES - v.shape[0])).reshape(1, LANES)
    q_lat, q_idx, ckv, misc = _dsa_proj(
        x, w_in_pad, q_g.reshape(1, -1), kv_g.reshape(1, -1), pad_lane(ki_g), pad_lane(ki_b),
        w_uq.astype(BF16), w_uk.astype(BF16), w_qidx_pad.astype(BF16))
    lane = jnp.arange(LANES)
    kidx = jnp.where(lane < IDX_DIM, misc, 0.0).astype(BF16)
    kidx_t = kidx.reshape(batch, seq, LANES).transpose(0, 2, 1)
    ckv_t = ckv.reshape(batch, seq, DSA_KV_RANK).transpose(0, 2, 1)
    o_lat = _dsa_attn(q_idx, misc, q_lat, kidx_t, ckv_t, ckv, batch, seq)
    return _dsa_out_ln(o_lat, w_uv.astype(BF16), w_out.astype(BF16), x, g, b, alpha)


def kernel(x, ln_g, ln_b, hg_w_in, hg_lower_bounds, hg_norm_g, hg_w_out, dsa_w_in, dsa_q_norm_g, dsa_kv_norm_g, dsa_w_uq, dsa_w_uk, dsa_w_uv, dsa_w_qidx, dsa_kidx_norm_g, dsa_kidx_norm_b, dsa_w_out, ffn_w_gate_up, ffn_w_down, moe_w_router, moe_w_gate_up, moe_w_down):
    batch, seq, d = x.shape
    depth = ln_g.shape[0]
    alpha = (2 * depth) ** 0.25
    lb_all = jax.nn.softmax(hg_lower_bounds.astype(F32), axis=0)
    lb_all = jnp.cumsum(lb_all, axis=0) - lb_all[0]
    h = x.reshape(batch * seq, d)
    row = lambda v: v.reshape(1, d)
    for layer in range(depth):
        j = layer // 2
        g0, b0, g1, b1 = row(ln_g[layer, 0]), row(ln_b[layer, 0]), row(ln_g[layer, 1]), row(ln_b[layer, 1])
        if layer % 2 == 0:
            q, k, lf, v, g = _hgrn_proj(h, hg_w_in[j].astype(BF16), row(lb_all[j]))
            o = _hgrn_rec(q, k, lf, v, g, hg_norm_g[j], batch, seq)
            h = _proj_ln(o, hg_w_out[j].astype(BF16), h, g0, b0, alpha)
            h = _ffn_ln(h, ffn_w_gate_up[j].astype(BF16), ffn_w_down[j].astype(BF16), g1, b1, alpha)
        else:
            h = _dsa_layer(h, batch, seq, dsa_w_in[j], dsa_q_norm_g[j], dsa_kv_norm_g[j], dsa_w_uq[j],
                           dsa_w_uk[j], dsa_w_uv[j], dsa_w_qidx[j], dsa_kidx_norm_g[j], dsa_kidx_norm_b[j],
                           dsa_w_out[j], g0, b0, alpha)
            w_router_pad = jnp.pad(moe_w_router[j], ((0, 0), (0, LANES - N_EXPERTS)))
            h = _moe_layer(h, w_router_pad, moe_w_gate_up[j].astype(BF16), moe_w_down[j].astype(BF16),
                           g1, b1, alpha)
    return h.reshape(batch, seq, d)
```

```python
import functools

import numpy as np
import jax
import jax.numpy as jnp
from jax import lax
from jax.experimental import pallas as pl
from jax.experimental.pallas import tpu as pltpu

F32, BF16, I32 = jnp.float32, jnp.bfloat16, jnp.int32

HG_HEADS = 8
DSA_HEADS = 8
DSA_HEAD_DIM = 128
DSA_Q_RANK = 256
DSA_KV_RANK = 256
IDX_HEADS = 8
IDX_DIM = 64
IDX_TOPK_MAX = 256
N_EXPERTS = 8
LN_EPS = 1e-5
RMS_EPS = 1e-6
NEG_BIG = -1e30

LANES = 128
SUBLANES = 8
V7X_VMEM_BYTES = 64 * 1024 * 1024
VMEM_LIMIT = V7X_VMEM_BYTES * 7 // 8

ROW_TILE = 512
FFN_ROW_TILE = 1024
FFN_F_TILE = 512
MOE_ROW_TILE = 512
HG_TIME_TILE = 512
HG_CHUNK = 128
HG_HEAD_GROUP = 4
DSA_Q_TILE = 256
DSA_KEY_TILE = 512
DSA_SOFTMAX_ROWS = 128
DSA_GROUP_ROWS = 512
LOG2_E = 1.4426950408889634
GATHER_TILE = 512
COMBINE_TILE = 256
DMA_LOOP_UNROLL = 8


def _cparams(*sem):
    return pltpu.CompilerParams(dimension_semantics=sem, vmem_limit_bytes=VMEM_LIMIT)


def _tile(n, t):
    t = min(n, t)
    assert n % t == 0, (n, t)
    return t


def _dot(a, b):
    return jnp.dot(a, b, preferred_element_type=F32)


def _dot_nt(a, b):
    return lax.dot_general(a, b, (((1,), (1,)), ((), ())), preferred_element_type=F32)


def _dot_tn(a, b):
    return lax.dot_general(a, b, (((0,), (0,)), ((), ())), preferred_element_type=F32)


def _layer_norm(y, g, b):
    mu = jnp.mean(y, axis=-1, keepdims=True)
    yc = y - mu
    var = jnp.mean(yc * yc, axis=-1, keepdims=True)
    return yc * lax.rsqrt(var + LN_EPS) * g + b


def _row_spec(t, d):
    return pl.BlockSpec((t, d), lambda i: (i, 0))


def _const_spec(shape):
    return pl.BlockSpec(shape, lambda *_: (0,) * len(shape))


def _hgrn_proj_kernel(x_ref, w_ref, lb_ref, q_ref, k_ref, lf_ref, v_ref, g_ref):
    d = x_ref.shape[1]
    xb = x_ref[...].astype(BF16)
    lb = lb_ref[...]
    q_ref[...] = _dot(xb, w_ref[:, 0:d])
    fp = _dot(xb, w_ref[:, d:2 * d])
    lf_ref[...] = jnp.log(lb + (1.0 - lb) * jax.nn.sigmoid(fp))
    k_ref[...] = (1.0 - lb) * jax.nn.sigmoid(-fp)
    v_ref[...] = _dot(xb, w_ref[:, 2 * d:3 * d]).astype(BF16)
    g_ref[...] = _dot(xb, w_ref[:, 3 * d:4 * d])


def _hgrn_proj(x, w_in, lb):
    n, d = x.shape
    tm = _tile(n, ROW_TILE)
    outs = [jax.ShapeDtypeStruct((n, d), dt) for dt in (F32, F32, F32, BF16, F32)]
    return pl.pallas_call(
        _hgrn_proj_kernel,
        grid=(n // tm,),
        in_specs=[_row_spec(tm, d), _const_spec((d, 4 * d)), _const_spec((1, d))],
        out_specs=[_row_spec(tm, d)] * 5,
        out_shape=outs,
        compiler_params=_cparams("parallel"),
        name="hgrn_proj",
    )(x, w_in, lb)


def _cumsum_rows(x):
    c = x.shape[0]
    row = lax.broadcasted_iota(I32, x.shape, 0)
    sh = 1
    while sh < min(c, 8):
        x = x + jnp.where(row >= sh, pltpu.roll(x, sh, axis=0), 0.0)
        sh *= 2
    while sh < c:
        x = x + jnp.concatenate([jnp.zeros((sh, x.shape[1]), x.dtype), x[:c - sh, :]], axis=0)
        sh *= 2
    return x


def _level_decay(b, lf, m):
    c = b.shape[0]
    if m >= 4:
        parts = []
        for lo in range(0, c, 2 * m):
            parts.append(b[lo:lo + 2 * m, :] - b[lo + m - 1:lo + m, :])
        d = parts[0] if len(parts) == 1 else jnp.concatenate(parts, axis=0)
    else:
        row = lax.broadcasted_iota(I32, b.shape, 0)
        if m == 1:
            d = jnp.where((row & 1) == 0, 0.0, lf)
        else:
            ph = row & 3
            nxt = pltpu.roll(lf, c - 1, axis=0)
            prv = pltpu.roll(lf, 1, axis=0)
            d = jnp.where(ph == 0, nxt, jnp.where(ph == 1, 0.0, jnp.where(ph == 2, lf, lf + prv)))
    return jnp.exp(-jnp.abs(d))


def _hgrn_levels(chunk):
    levels = []
    m = chunk // 2
    while m >= 1:
        levels.append(m)
        m //= 2
    return levels


def _hgrn_masks(chunk):
    t = np.arange(chunk)[:, None]
    s = np.arange(chunk)[None, :]
    x = t ^ s
    masks = [((t > s) & (x >= m) & (x < 2 * m)) for m in _hgrn_levels(chunk)]
    masks.append(t == s)
    return jnp.asarray(np.stack(masks).astype(np.float32))


def _hgrn_rec_kernel(q_ref, k_ref, lf_ref, v_ref, g_ref, ng_ref, masks_ref, o_ref, st_ref, *, chunk, dh):
    hg = pl.program_id(1)
    heads = q_ref.shape[1] // dh
    levels = _hgrn_levels(chunk)

    @pl.when(pl.program_id(2) == 0)
    def _():
        st_ref[...] = jnp.zeros_like(st_ref)

    def head_chunk(sl, hh):
        ln = slice(hh * dh, (hh + 1) * dh)
        q, k, lf, v, g = q_ref[sl, ln], k_ref[sl, ln], lf_ref[sl, ln], v_ref[sl, ln], g_ref[sl, ln]
        b = _cumsum_rows(lf)
        s = masks_ref[len(levels)] * _dot_nt(q.astype(BF16), k.astype(BF16))
        for li, m in enumerate(levels):
            e = _level_decay(b, lf, m)
            s = s + masks_ref[li] * _dot_nt((q * e).astype(BF16), (k * e).astype(BF16))
        o = _dot(s.astype(BF16), v)
        st = st_ref[hh]
        b_last = b[chunk - 1:chunk, :]
        o = o + _dot_nt((q * jnp.exp(b)).astype(BF16), st.astype(BF16))
        kt = (k * jnp.exp(b_last - b)).astype(BF16)
        st_ref[hh] = st * jnp.exp(b_last) + _dot_tn(v, kt)
        ms = jnp.mean(o * o, axis=-1, keepdims=True)
        o = o * lax.rsqrt(ms + RMS_EPS) * ng_ref[pl.ds(hg * heads + hh, 1), :]
        o_ref[sl, ln] = (o * (g * jax.nn.sigmoid(g))).astype(o_ref.dtype)

    def body(c, carry):
        sl = pl.ds(pl.multiple_of(c * chunk, chunk), chunk)
        for hh in range(heads):
            head_chunk(sl, hh)
        return carry

    lax.fori_loop(0, q_ref.shape[0] // chunk, body, 0)


def _hgrn_rec(q, k, lf, v, g, norm_g, batch, seq):
    n, d = q.shape
    dh = d // HG_HEADS
    tt = _tile(seq, HG_TIME_TILE)
    chunk = _tile(tt, HG_CHUNK)
    nt = seq // tt
    masks = _hgrn_masks(chunk)
    blk = pl.BlockSpec((tt, HG_HEAD_GROUP * dh), lambda b, h, t: (b * nt + t, h))
    return pl.pallas_call(
        functools.partial(_hgrn_rec_kernel, chunk=chunk, dh=dh),
        grid=(batch, HG_HEADS // HG_HEAD_GROUP, nt),
        in_specs=[blk, blk, blk, blk, blk, _const_spec(norm_g.shape), _const_spec(masks.shape)],
        out_specs=blk,
        out_shape=jax.ShapeDtypeStruct((n, d), BF16),
        scratch_shapes=[pltpu.VMEM((HG_HEAD_GROUP, dh, dh), F32)],
        compiler_params=_cparams("parallel", "parallel", "arbitrary"),
        name="hgrn_rec",
    )(q, k, lf, v, g, norm_g, masks)


def _proj_ln_kernel(a_ref, w_ref, x_ref, g_ref, b_ref, o_ref, *, alpha):
    y = _dot(a_ref[...], w_ref[...])
    o_ref[...] = _layer_norm(alpha * x_ref[...] + y, g_ref[...], b_ref[...])


def _proj_ln(a, w, x, g, b, alpha):
    n, d = x.shape
    ka = a.shape[1]
    tm = _tile(n, ROW_TILE)
    return pl.pallas_call(
        functools.partial(_proj_ln_kernel, alpha=alpha),
        grid=(n // tm,),
        in_specs=[_row_spec(tm, ka), _const_spec((ka, d)), _row_spec(tm, d), _const_spec((1, d)),
                  _const_spec((1, d))],
        out_specs=_row_spec(tm, d),
        out_shape=jax.ShapeDtypeStruct((n, d), F32),
        compiler_params=_cparams("parallel"),
        name="proj_ln",
    )(a, w, x, g, b)


def _swiglu_step(xb, wg_ref, wu_ref, wd_ref):
    gate = _dot(xb, wg_ref[...])
    up = _dot(xb, wu_ref[...])
    act = (gate * jax.nn.sigmoid(gate) * up).astype(BF16)
    return _dot(act, wd_ref[...])


def _ffn_ln_kernel(x_ref, wg_ref, wu_ref, wd_ref, g_ref, b_ref, o_ref, acc_ref, *, alpha):
    f = pl.program_id(1)
    y = _swiglu_step(x_ref[...].astype(BF16), wg_ref, wu_ref, wd_ref)

    @pl.when(f == 0)
    def _():
        acc_ref[...] = y

    @pl.when(f > 0)
    def _():
        acc_ref[...] += y

    @pl.when(f == pl.num_programs(1) - 1)
    def _():
        o_ref[...] = _layer_norm(alpha * x_ref[...] + acc_ref[...], g_ref[...], b_ref[...])


def _ffn_ln(x, w_gate_up, w_down, g, b, alpha):
    n, d = x.shape
    ff = w_down.shape[0]
    tm = _tile(n, FFN_ROW_TILE)
    tf = _tile(ff, FFN_F_TILE)
    nf = ff // tf
    return pl.pallas_call(
        functools.partial(_ffn_ln_kernel, alpha=alpha),
        grid=(n // tm, nf),
        in_specs=[pl.BlockSpec((tm, d), lambda i, f: (i, 0)),
                  pl.BlockSpec((d, tf), lambda i, f: (0, f)),
                  pl.BlockSpec((d, tf), lambda i, f: (0, f + nf)),
                  pl.BlockSpec((tf, d), lambda i, f: (f, 0)),
                  _const_spec((1, d)), _const_spec((1, d))],
        out_specs=pl.BlockSpec((tm, d), lambda i, f: (i, 0)),
        out_shape=jax.ShapeDtypeStruct((n, d), F32),
        scratch_shapes=[pltpu.VMEM((tm, d), F32)],
        compiler_params=_cparams("parallel", "arbitrary"),
        name="ffn_ln",
    )(x, w_gate_up, w_gate_up, w_down, g, b)


def _rows_from_tiles(t_ref):
    return jnp.concatenate([t_ref[:, c, :] for c in range(t_ref.shape[1])], axis=1)


def _rows_to_tiles(t_ref, y):
    for c in range(t_ref.shape[1]):
        t_ref[:, c, :] = y[:, c * LANES:(c + 1) * LANES]


def _moe_ffn_kernel(te_ref, nt_ref, x_ref, wg_ref, wu_ref, wd_ref, o_ref, xb_ref, acc_ref):
    i, f = pl.program_id(0), pl.program_id(1)
    live = i < nt_ref[0]
    last = f == pl.num_programs(1) - 1

    @pl.when(live)
    def _():
        @pl.when(f == 0)
        def _():
            xb_ref[...] = _rows_from_tiles(x_ref).astype(BF16)

        y = _swiglu_step(xb_ref[...], wg_ref, wu_ref, wd_ref)

        @pl.when(f == 0)
        def _():
            acc_ref[...] = y

        @pl.when(f > 0)
        def _():
            acc_ref[...] += y

        @pl.when(last)
        def _():
            _rows_to_tiles(o_ref, acc_ref[...])

    @pl.when(jnp.logical_not(live) & last)
    def _():
        o_ref[...] = jnp.zeros_like(o_ref)


def _moe_ffn(tile_expert, n_live, xs, w_gate_up, w_down, tm):
    p, sub, lanes = xs.shape
    d = sub * lanes
    ff = w_down.shape[1]
    assert p % tm == 0
    tf = _tile(ff, FFN_F_TILE)
    nf = ff // tf
    tile_blk = pl.BlockSpec((tm, sub, lanes), lambda i, f, te, nt: (i, 0, 0))
    grid_spec = pltpu.PrefetchScalarGridSpec(
        num_scalar_prefetch=2,
        grid=(p // tm, nf),
        in_specs=[tile_blk,
                  pl.BlockSpec((None, d, tf), lambda i, f, te, nt: (te[i], 0, f)),
                  pl.BlockSpec((None, d, tf), lambda i, f, te, nt: (te[i], 0, f + nf)),
                  pl.BlockSpec((None, tf, d), lambda i, f, te, nt: (te[i], f, 0))],
        out_specs=tile_blk,
        scratch_shapes=[pltpu.VMEM((tm, d), BF16), pltpu.VMEM((tm, d), F32)],
    )
    return pl.pallas_call(
        _moe_ffn_kernel,
        grid_spec=grid_spec,
        out_shape=jax.ShapeDtypeStruct(xs.shape, F32),
        compiler_params=_cparams("parallel", "arbitrary"),
        name="moe_ffn",
    )(tile_expert, n_live, xs, w_gate_up, w_gate_up, w_down)


def _router_kernel(x_ref, w_ref, tril_ref, gate_ref, idx_ref, cnt_ref, carry_ref):
    @pl.when(pl.program_id(0) == 0)
    def _():
        carry_ref[...] = jnp.zeros_like(carry_ref)

    logits = jnp.dot(x_ref[...], w_ref[...], preferred_element_type=F32, precision=lax.Precision.HIGHEST)
    lane = lax.broadcasted_iota(I32, logits.shape, 1)
    l1 = jnp.where(lane < N_EXPERTS, logits, -jnp.inf)
    m1 = jnp.max(l1, axis=-1, keepdims=True)
    i1 = jnp.min(jnp.where(l1 == m1, lane, LANES), axis=-1, keepdims=True)
    l2 = jnp.where(lane == i1, -jnp.inf, l1)
    m2 = jnp.max(l2, axis=-1, keepdims=True)
    i2 = jnp.min(jnp.where(l2 == m2, lane, LANES), axis=-1, keepdims=True)
    e = jnp.exp(m2 - m1)
    den = 1.0 + e
    gate_ref[...] = jnp.where(lane == 0, 1.0 / den, jnp.where(lane == 1, e / den, 0.0))
    oh1 = jnp.where(lane == i1, 1.0, 0.0)
    oh2 = jnp.where(lane == i2, 1.0, 0.0)
    oh = oh1 + oh2
    before = _dot(tril_ref[...], oh.astype(BF16)) + carry_ref[...]
    r1 = jnp.sum(oh1 * before, axis=-1, keepdims=True).astype(I32)
    r2 = jnp.sum(oh2 * before, axis=-1, keepdims=True).astype(I32)
    carry_ref[...] += jnp.sum(oh, axis=0, keepdims=True)
    cnt_ref[...] = carry_ref[...]
    idx_ref[...] = jnp.where(lane == 0, i1, jnp.where(lane == 1, i2, jnp.where(lane == 2, r1, jnp.where(
        lane == 3, r2, 0))))


def _router(x, w_router_pad):
    n, d = x.shape
    tm = _tile(n, ROW_TILE)
    tril = jnp.asarray(np.tril(np.ones((tm, tm), np.float32), -1), BF16)
    return pl.pallas_call(
        _router_kernel,
        grid=(n // tm,),
        in_specs=[_row_spec(tm, d), _const_spec((d, LANES)), _const_spec((tm, tm))],
        out_specs=[_row_spec(tm, LANES), _row_spec(tm, LANES), _const_spec((1, LANES))],
        out_shape=[jax.ShapeDtypeStruct((n, LANES), F32), jax.ShapeDtypeStruct((n, LANES), I32),
                   jax.ShapeDtypeStruct((1, LANES), F32)],
        scratch_shapes=[pltpu.VMEM((1, LANES), F32)],
        compiler_params=_cparams("arbitrary"),
        name="router",
    )(x, w_router_pad, tril)


def _moe_dispatch_kernel(p1_ref, p2_ref, x_ref, xs_init_hbm, xs_hbm, stage_ref, sem):
    del xs_init_hbm
    td = x_ref.shape[0]
    base = pl.program_id(0) * td
    _rows_to_tiles(stage_ref, x_ref[...])

    def copies(r, row1, row2):
        return (pltpu.make_async_copy(stage_ref.at[r], xs_hbm.at[row1], sem),
                pltpu.make_async_copy(stage_ref.at[r], xs_hbm.at[row2], sem))

    def issue(r, c):
        for cp in copies(r, p1_ref[base + r], p2_ref[base + r]):
            cp.start()
        return c

    def drain(r, c):
        for cp in copies(r, 0, 0):
            cp.wait()
        return c

    lax.fori_loop(0, td, issue, 0, unroll=DMA_LOOP_UNROLL)
    lax.fori_loop(0, td, drain, 0, unroll=DMA_LOOP_UNROLL)


def _moe_dispatch(p1, p2, x, p_rows):
    n, d = x.shape
    sub = d // LANES
    td = _tile(n, GATHER_TILE)
    xs_init = jnp.zeros((p_rows, sub, LANES), F32)
    grid_spec = pltpu.PrefetchScalarGridSpec(
        num_scalar_prefetch=2,
        grid=(n // td,),
        in_specs=[pl.BlockSpec((td, d), lambda i, a, c: (i, 0)), pl.BlockSpec(memory_space=pl.ANY)],
        out_specs=pl.BlockSpec(memory_space=pl.ANY),
        scratch_shapes=[pltpu.VMEM((td, sub, LANES), F32), pltpu.SemaphoreType.DMA(())],
    )
    return pl.pallas_call(
        _moe_dispatch_kernel,
        grid_spec=grid_spec,
        out_shape=jax.ShapeDtypeStruct(xs_init.shape, F32),
        input_output_aliases={3: 0},
        compiler_params=_cparams("arbitrary"),
        name="moe_dispatch",
    )(p1, p2, x, xs_init)


def _moe_combine_ln_kernel(p1_ref, p2_ref, x_ref, gate_ref, ys_hbm, g_ref, b_ref, o_ref, buf_ref, sem, *, alpha):
    i, nsteps = pl.program_id(0), pl.num_programs(0)
    tc = o_ref.shape[0]

    def copies(slot, r, row1, row2):
        return (pltpu.make_async_copy(ys_hbm.at[row1], buf_ref.at[slot, 0, r], sem.at[slot]),
                pltpu.make_async_copy(ys_hbm.at[row2], buf_ref.at[slot, 1, r], sem.at[slot]))

    def issue(step, slot):
        def body(r, c):
            for cp in copies(slot, r, p1_ref[step * tc + r], p2_ref[step * tc + r]):
                cp.start()
            return c

        lax.fori_loop(0, tc, body, 0, unroll=DMA_LOOP_UNROLL)

    @pl.when(i == 0)
    def _():
        issue(0, 0)

    @pl.when(i + 1 < nsteps)
    def _():
        issue(i + 1, (i + 1) % 2)

    slot = i % 2

    def drain(r, c):
        for cp in copies(slot, r, 0, 0):
            cp.wait()
        return c

    lax.fori_loop(0, tc, drain, 0, unroll=DMA_LOOP_UNROLL)
    w = gate_ref[...]
    y = w[:, 0:1] * _rows_from_tiles(buf_ref.at[slot, 0]) + w[:, 1:2] * _rows_from_tiles(buf_ref.at[slot, 1])
    o_ref[...] = _layer_norm(alpha * x_ref[...] + y, g_ref[...], b_ref[...])


def _moe_combine_ln(p1, p2, x, gate, ys, g, b, alpha):
    n, d = x.shape
    sub = d // LANES
    tc = _tile(n, COMBINE_TILE)
    grid_spec = pltpu.PrefetchScalarGridSpec(
        num_scalar_prefetch=2,
        grid=(n // tc,),
        in_specs=[pl.BlockSpec((tc, d), lambda i, a, c: (i, 0)),
                  pl.BlockSpec((tc, LANES), lambda i, a, c: (i, 0)),
                  pl.BlockSpec(memory_space=pl.ANY),
                  pl.BlockSpec((1, d), lambda i, a, c: (0, 0)),
                  pl.BlockSpec((1, d), lambda i, a, c: (0, 0))],
        out_specs=pl.BlockSpec((tc, d), lambda i, a, c: (i, 0)),
        scratch_shapes=[pltpu.VMEM((2, 2, tc, sub, LANES), F32), pltpu.SemaphoreType.DMA((2,))],
    )
    return pl.pallas_call(
        functools.partial(_moe_combine_ln_kernel, alpha=alpha),
        grid_spec=grid_spec,
        out_shape=jax.ShapeDtypeStruct((n, d), F32),
        compiler_params=_cparams("arbitrary"),
        name="moe_combine_ln",
    )(p1, p2, x, gate, ys, g, b)


def _moe_layer(x, w_router_pad, w_gate_up, w_down, g, b, alpha):
    n = x.shape[0]
    gate, idx, cnt = _router(x, w_router_pad)
    tm = min(MOE_ROW_TILE, 2 * n)
    p_rows = 2 * n + N_EXPERTS * tm
    counts = cnt[0, :N_EXPERTS].astype(I32)
    padded = ((counts + tm - 1) // tm) * tm
    pends = jnp.cumsum(padded)
    pstarts = pends - padded
    p1 = pstarts[idx[:, 0]] + idx[:, 2]
    p2 = pstarts[idx[:, 1]] + idx[:, 3]
    tile_start = jnp.arange(p_rows // tm, dtype=I32) * tm
    tile_expert = jnp.minimum(jnp.sum(tile_start[:, None] >= pends[None, :], axis=1), N_EXPERTS - 1).astype(I32)
    n_live = (pends[-1] // tm).astype(I32).reshape(1)
    xs = _moe_dispatch(p1, p2, x, p_rows)
    ys = _moe_ffn(tile_expert, n_live, xs, w_gate_up, w_down, tm)
    return _moe_combine_ln(p1, p2, x, gate, ys, g, b, alpha)


def _rms_norm(x, g):
    ms = jnp.mean(x * x, axis=-1, keepdims=True)
    return x * lax.rsqrt(ms + RMS_EPS) * g


def _dsa_proj_kernel(x_ref, win_ref, qg_ref, kvg_ref, kig_ref, kib_ref, wuq_ref, wuk_ref, wqi_ref,
                     qlat_ref, qidx_ref, ckv_ref, misc_ref, *, w_scale):
    xb = x_ref[...].astype(BF16)
    qr, kr = DSA_Q_RANK, DSA_KV_RANK
    cq = _rms_norm(_dot(xb, win_ref[:, 0:qr]), qg_ref[...]).astype(BF16)
    ckv_ref[...] = _rms_norm(_dot(xb, win_ref[:, qr:qr + kr]), kvg_ref[...]).astype(ckv_ref.dtype)
    misc = _dot(xb, win_ref[:, qr + kr:qr + kr + LANES])
    lane = lax.broadcasted_iota(I32, misc.shape, 1)
    is_key = lane < IDX_DIM
    mu = jnp.sum(jnp.where(is_key, misc, 0.0), axis=-1, keepdims=True) * (1.0 / IDX_DIM)
    kc = jnp.where(is_key, misc - mu, 0.0)
    var = jnp.sum(kc * kc, axis=-1, keepdims=True) * (1.0 / IDX_DIM)
    kn = kc * lax.rsqrt(var + LN_EPS) * kig_ref[...] + kib_ref[...]
    misc_ref[...] = jnp.where(is_key, kn, misc * w_scale)
    q = _dot(cq, wuq_ref[...])
    hd = DSA_HEAD_DIM
    for h in range(DSA_HEADS):
        qidx_ref[h] = _dot(cq, wqi_ref[:, h * LANES:(h + 1) * LANES]).astype(qidx_ref.dtype)
        qlat_ref[h] = _dot(q[:, h * hd:(h + 1) * hd].astype(BF16), wuk_ref[h]).astype(qlat_ref.dtype)


def _head_rows_spec(t, wd):
    return pl.BlockSpec((DSA_HEADS, t, wd), lambda i: (0, i, 0))


def _dsa_proj(x, w_in_pad, q_g, kv_g, ki_g, ki_b, w_uq, w_uk, w_qidx_pad):
    n, d = x.shape
    tm = _tile(n, ROW_TILE)
    w_scale = IDX_HEADS ** -0.5 * IDX_DIM ** -0.5
    outs = [jax.ShapeDtypeStruct((DSA_HEADS, n, DSA_KV_RANK), BF16),
            jax.ShapeDtypeStruct((IDX_HEADS, n, LANES), BF16),
            jax.ShapeDtypeStruct((n, DSA_KV_RANK), BF16),
            jax.ShapeDtypeStruct((n, LANES), F32)]
    ins = [x, w_in_pad, q_g, kv_g, ki_g, ki_b, w_uq, w_uk, w_qidx_pad]
    return pl.pallas_call(
        functools.partial(_dsa_proj_kernel, w_scale=w_scale),
        grid=(n // tm,),
        in_specs=[_row_spec(tm, d)] + [_const_spec(a.shape) for a in ins[1:]],
        out_specs=[_head_rows_spec(tm, DSA_KV_RANK), _head_rows_spec(tm, LANES), _row_spec(tm, DSA_KV_RANK),
                   _row_spec(tm, LANES)],
        out_shape=outs,
        compiler_params=_cparams("parallel"),
        name="dsa_proj",
    )(*ins)


def _order_key(x):
    k = pltpu.bitcast(x, I32)
    k = k ^ ((k >> 31) & 0x7FFFFFFF)
    return jnp.where(x == 0.0, 0, k)


def _dsa_attn_kernel(qidx_ref, wt_ref, qlat_ref, kidx_ref, ckv_t_ref, ckv_ref, o_ref,
                     key_ref, bias_ref, s_ref, p_ref, m_ref, l_ref, alpha_ref, acc_ref,
                     *, topk, scale, tk, rc, rg):
    nh, tq, kr = qlat_ref.shape
    seq = ckv_ref.shape[0]
    q0 = pl.program_id(1) * tq
    n_tiles = (q0 + tq + tk - 1) // tk
    kf = float(topk)
    int_min = jnp.iinfo(jnp.int32).min
    nbits = max(1, (seq - 1).bit_length())
    w_t = wt_ref[...]
    q_lane = q0 + lax.broadcasted_iota(I32, (tk, tq), 1)
    key_row = lax.broadcasted_iota(I32, (tk, tq), 0)
    sub = lax.broadcasted_iota(I32, (SUBLANES, tq), 0)

    def key_tile(kb):
        return pl.ds(pl.multiple_of(kb * tk, tk), tk)

    q_idx_all = qidx_ref[...].reshape(nh * tq, LANES)

    def score_tile(kb, c):
        r = _dot_nt(kidx_ref[key_tile(kb), :], q_idx_all)
        score = jnp.zeros((tk, tq), F32)
        for h in range(nh):
            score = score + jnp.maximum(r[:, h * tq:(h + 1) * tq], 0.0) * w_t[h:h + 1, :]
        score = jnp.where(kb * tk + key_row <= q_lane, score, NEG_BIG)
        key_ref[key_tile(kb), :] = _order_key(score)
        return c

    lax.fori_loop(0, n_tiles, score_tile, 0)

    def count_keys(pred):
        n_acc = 4

        def tile(kb, accs):
            blk = key_ref[key_tile(kb), :]
            accs = list(accs)
            for s8 in range(tk // SUBLANES):
                hit = pred(blk[s8 * SUBLANES:(s8 + 1) * SUBLANES, :], kb * tk + s8 * SUBLANES)
                accs[s8 % n_acc] = accs[s8 % n_acc] + jnp.where(hit, 1.0, 0.0)
            return tuple(accs)

        accs = lax.fori_loop(0, n_tiles, tile, tuple(jnp.zeros((SUBLANES, tq), F32) for _ in range(n_acc)))
        return jnp.sum((accs[0] + accs[1]) + (accs[2] + accs[3]), axis=0, keepdims=True)

    def rep(row):
        return jnp.broadcast_to(row, (SUBLANES, tq))

    def value_step(it, t):
        cand = t + lax.shift_left(jnp.int32(1), 31 - it)
        cand_b = rep(cand)
        return jnp.where(count_keys(lambda k, r0: k >= cand_b) >= kf, cand, t)

    t_all = lax.fori_loop(0, 32, value_step, jnp.full((1, tq), int_min, I32))
    t_b = rep(t_all)
    need = kf - count_keys(lambda k, r0: k > t_b)
    extra = count_keys(lambda k, r0: k == t_b) - need

    def last_kept_tie():
        def index_step(it, j):
            cand = j + lax.shift_left(jnp.int32(1), nbits - 1 - it)
            cand_b = rep(cand)
            below = count_keys(lambda k, r0: (k == t_b) & (r0 + sub < cand_b))
            return jnp.where(below < need, cand, j)

        return lax.fori_loop(0, nbits, index_step, jnp.zeros((1, tq), I32))

    j_all = lax.cond(jnp.max(extra) > 0.0, last_kept_tie, lambda: jnp.full((1, tq), seq, I32))
    q_lat_all = qlat_ref[...].reshape(nh * tq, kr)
    m_ref[...] = jnp.full(m_ref.shape, NEG_BIG, F32)
    l_ref[...] = jnp.zeros(l_ref.shape, F32)
    acc_ref[...] = jnp.zeros(acc_ref.shape, F32)
    c2 = scale * LOG2_E

    def attn_tile(kb, c):
        key = key_ref[key_tile(kb), :]
        krow = kb * tk + key_row
        sel = ((key > t_all) | ((key == t_all) & (krow <= j_all))) & (krow <= q_lane)
        bias_ref[...] = jnp.where(sel, 0.0, NEG_BIG).T
        ckv_t = ckv_t_ref[:, key_tile(kb)]
        ckv = ckv_ref[key_tile(kb), :]
        for g0 in range(0, nh * tq, rg):
            grp = slice(g0, g0 + rg)
            s_ref[grp, :] = _dot(q_lat_all[grp, :], ckv_t)
            for r0 in range(g0, g0 + rg, rc):
                rows = slice(r0, r0 + rc)
                s = s_ref[rows, :] + bias_ref[r0 % tq:r0 % tq + rc, :]
                m_prev = m_ref[rows, :]
                m_new = jnp.maximum(m_prev, jnp.max(s, axis=-1, keepdims=True))
                alpha = jnp.exp2((m_prev - m_new) * c2)
                p_lanes = None
                for g in range(tk // LANES):
                    lanes = slice(g * LANES, (g + 1) * LANES)
                    p = jnp.exp2((s[:, lanes] - m_new) * c2)
                    p_ref[rows, lanes] = p.astype(BF16)
                    p_lanes = p if p_lanes is None else p_lanes + p
                l_ref[rows, :] = alpha * l_ref[rows, :] + p_lanes
                m_ref[rows, :] = m_new
                alpha_ref[rows, :] = alpha
            pv = _dot(p_ref[grp, :], ckv)
            a = alpha_ref[grp, :]
            for g in range(kr // LANES):
                lanes = slice(g * LANES, (g + 1) * LANES)
                acc_ref[grp, lanes] = a * acc_ref[grp, lanes] + pv[:, lanes]
        return c

    lax.fori_loop(0, n_tiles, attn_tile, 0)
    den = jnp.sum(l_ref[...], axis=-1, keepdims=True)
    o_ref[...] = (acc_ref[...] / den).reshape(nh, tq, kr).astype(o_ref.dtype)


def _dsa_attn(q_idx, w_t, q_lat, kidx, ckv_t, ckv, batch, seq):
    nh, n, kr = q_lat.shape
    tq = _tile(seq, DSA_Q_TILE)
    tk = _tile(seq, DSA_KEY_TILE)
    rc = _tile(tq, DSA_SOFTMAX_ROWS)
    nq = seq // tq
    topk = min(IDX_TOPK_MAX, seq // 4)
    heads = lambda wd: pl.BlockSpec((nh, tq, wd), lambda b, i: (0, b * nq + i, 0))
    stat = pltpu.VMEM((nh * tq, LANES), F32)
    return pl.pallas_call(
        functools.partial(_dsa_attn_kernel, topk=topk, scale=DSA_HEAD_DIM ** -0.5, tk=tk, rc=rc,
                          rg=_tile(nh * tq, DSA_GROUP_ROWS)),
        grid=(batch, nq),
        in_specs=[heads(LANES), pl.BlockSpec((None, IDX_HEADS, tq), lambda b, i: (b, 0, i)), heads(kr),
                  pl.BlockSpec((seq, LANES), lambda b, i: (b, 0)),
                  pl.BlockSpec((None, kr, seq), lambda b, i: (b, 0, 0)),
                  pl.BlockSpec((seq, kr), lambda b, i: (b, 0))],
        out_specs=heads(kr),
        out_shape=jax.ShapeDtypeStruct((nh, n, kr), BF16),
        scratch_shapes=[pltpu.VMEM((seq, tq), I32), pltpu.VMEM((tq, tk), F32),
                        pltpu.VMEM((nh * tq, tk), F32), pltpu.VMEM((nh * tq, tk), BF16),
                        stat, stat, stat, pltpu.VMEM((nh * tq, kr), F32)],
        compiler_params=_cparams("parallel", "parallel"),
        name="dsa_attn",
    )(q_idx, w_t, q_lat, kidx, ckv_t, ckv)


def _dsa_out_ln_kernel(olat_ref, wuv_ref, wout_ref, x_ref, g_ref, b_ref, o_ref, *, alpha):
    hd = DSA_HEAD_DIM
    y = None
    for h in range(DSA_HEADS):
        oh = _dot(olat_ref[h], wuv_ref[h]).astype(BF16)
        yh = _dot(oh, wout_ref[h * hd:(h + 1) * hd, :])
        y = yh if y is None else y + yh
    o_ref[...] = _layer_norm(alpha * x_ref[...] + y, g_ref[...], b_ref[...])


def _dsa_out_ln(o_lat, w_uv, w_out, x, g, b, alpha):
    n, d = x.shape
    tm = _tile(n, ROW_TILE)
    return pl.pallas_call(
        functools.partial(_dsa_out_ln_kernel, alpha=alpha),
        grid=(n // tm,),
        in_specs=[_head_rows_spec(tm, o_lat.shape[2]), _const_spec(w_uv.shape), _const_spec(w_out.shape),
                  _row_spec(tm, d), _const_spec((1, d)), _const_spec((1, d))],
        out_specs=_row_spec(tm, d),
        out_shape=jax.ShapeDtypeStruct((n, d), F32),
        compiler_params=_cparams("parallel"),
        name="dsa_out_ln",
    )(o_lat, w_uv, w_out, x, g, b)


def _dsa_layer(x, batch, seq, w_in, q_g, kv_g, w_uq, w_uk, w_uv, w_qidx, ki_g, ki_b, w_out, g, b, alpha):
    d = x.shape[1]
    qk = DSA_Q_RANK + DSA_KV_RANK
    tail = w_in.shape[1] - qk
    w_in_pad = jnp.pad(w_in, ((0, 0), (0, LANES - tail))).astype(BF16)
    w_qidx_pad = jnp.pad(w_qidx.reshape(DSA_Q_RANK, IDX_HEADS, IDX_DIM),
                         ((0, 0), (0, 0), (0, LANES - IDX_DIM))).reshape(DSA_Q_RANK, IDX_HEADS * LANES)
    pad_lane = lambda v: jnp.pad(v, (0, LANES - v.shape[0])).reshape(1, LANES)
    q_lat, q_idx, ckv, misc = _dsa_proj(
        x, w_in_pad, q_g.reshape(1, -1), kv_g.reshape(1, -1), pad_lane(ki_g), pad_lane(ki_b),
        w_uq.astype(BF16), w_uk.astype(BF16), w_qidx_pad.astype(BF16))
    lane = jnp.arange(LANES)
    kidx = jnp.where(lane < IDX_DIM, misc, 0.0).astype(BF16)
    w_t = misc[:, IDX_DIM:IDX_DIM + IDX_HEADS].reshape(batch, seq, IDX_HEADS).transpose(0, 2, 1)
    ckv_t = ckv.reshape(batch, seq, DSA_KV_RANK).transpose(0, 2, 1)
    o_lat = _dsa_attn(q_idx, w_t, q_lat, kidx, ckv_t, ckv, batch, seq)
    return _dsa_out_ln(o_lat, w_uv.astype(BF16), w_out.astype(BF16), x, g, b, alpha)


def kernel(x, ln_g, ln_b, hg_w_in, hg_lower_bounds, hg_norm_g, hg_w_out, dsa_w_in, dsa_q_norm_g, dsa_kv_norm_g, dsa_w_uq, dsa_w_uk, dsa_w_uv, dsa_w_qidx, dsa_kidx_norm_g, dsa_kidx_norm_b, dsa_w_out, ffn_w_gate_up, ffn_w_down, moe_w_router, moe_w_gate_up, moe_w_down):
    batch, seq, d = x.shape
    depth = ln_g.shape[0]
    alpha = (2 * depth) ** 0.25
    lb_all = jax.nn.softmax(hg_lower_bounds.astype(F32), axis=0)
    lb_all = jnp.cumsum(lb_all, axis=0) - lb_all[0]
    h = x.reshape(batch * seq, d)
    row = lambda v: v.reshape(1, d)
    for layer in range(depth):
        j = layer // 2
        g0, b0, g1, b1 = row(ln_g[layer, 0]), row(ln_b[layer, 0]), row(ln_g[layer, 1]), row(ln_b[layer, 1])
        if layer % 2 == 0:
            q, k, lf, v, g = _hgrn_proj(h, hg_w_in[j].astype(BF16), row(lb_all[j]))
            o = _hgrn_rec(q, k, lf, v, g, hg_norm_g[j], batch, seq)
            h = _proj_ln(o, hg_w_out[j].astype(BF16), h, g0, b0, alpha)
            h = _ffn_ln(h, ffn_w_gate_up[j].astype(BF16), ffn_w_down[j].astype(BF16), g1, b1, alpha)
        else:
            h = _dsa_layer(h, batch, seq, dsa_w_in[j], dsa_q_norm_g[j], dsa_kv_norm_g[j], dsa_w_uq[j],
                           dsa_w_uk[j], dsa_w_uv[j], dsa_w_qidx[j], dsa_kidx_norm_g[j], dsa_kidx_norm_b[j],
                           dsa_w_out[j], g0, b0, alpha)
            w_router_pad = jnp.pad(moe_w_router[j], ((0, 0), (0, LANES - N_EXPERTS)))
            h = _moe_layer(h, w_router_pad, moe_w_gate_up[j].astype(BF16), moe_w_down[j].astype(BF16),
                           g1, b1, alpha)
    return h.reshape(batch, seq, d)
```

```python
import functools

import numpy as np
import jax
import jax.numpy as jnp
from jax import lax
from jax.experimental import pallas as pl
from jax.experimental.pallas import tpu as pltpu

F32, BF16, I32 = jnp.float32, jnp.bfloat16, jnp.int32

HG_HEADS = 8
DSA_HEADS = 8
DSA_HEAD_DIM = 128
DSA_Q_RANK = 256
DSA_KV_RANK = 256
IDX_HEADS = 8
IDX_DIM = 64
IDX_TOPK_MAX = 256
N_EXPERTS = 8
LN_EPS = 1e-5
RMS_EPS = 1e-6
NEG_BIG = -1e30

LANES = 128
SUBLANES = 8
WORD_BITS = 32
V7X_VMEM_BYTES = 64 * 1024 * 1024
VMEM_LIMIT = V7X_VMEM_BYTES * 7 // 8

ROW_TILE = 512
FFN_ROW_TILE = 1024
FFN_F_TILE = 512
MOE_ROW_TILE = 512
HG_TIME_TILE = 512
HG_CHUNK = 128
HG_HEAD_GROUP = 4
DSA_Q_TILE = 256
DSA_KEY_TILE = 512
DSA_SOFTMAX_ROWS = 128
DSA_GROUP_ROWS = 512
LOG2_E = 1.4426950408889634
GATHER_TILE = 512
COMBINE_TILE = 256
DMA_LOOP_UNROLL = 8


def _cparams(*sem):
    return pltpu.CompilerParams(dimension_semantics=sem, vmem_limit_bytes=VMEM_LIMIT)


def _tile(n, t):
    t = min(n, t)
    assert n % t == 0, (n, t)
    return t


def _dot(a, b):
    return jnp.dot(a, b, preferred_element_type=F32)


def _dot_nt(a, b):
    return lax.dot_general(a, b, (((1,), (1,)), ((), ())), preferred_element_type=F32)


def _dot_tn(a, b):
    return lax.dot_general(a, b, (((0,), (0,)), ((), ())), preferred_element_type=F32)


def _layer_norm(y, g, b):
    mu = jnp.mean(y, axis=-1, keepdims=True)
    yc = y - mu
    var = jnp.mean(yc * yc, axis=-1, keepdims=True)
    return yc * lax.rsqrt(var + LN_EPS) * g + b


def _row_spec(t, d):
    return pl.BlockSpec((t, d), lambda i: (i, 0))


def _const_spec(shape):
    return pl.BlockSpec(shape, lambda *_: (0,) * len(shape))


def _hgrn_proj_kernel(x_ref, w_ref, lb_ref, q_ref, k_ref, lf_ref, v_ref, g_ref):
    d = x_ref.shape[1]
    xb = x_ref[...].astype(BF16)
    lb = lb_ref[...]
    q_ref[...] = _dot(xb, w_ref[:, 0:d])
    fp = _dot(xb, w_ref[:, d:2 * d])
    lf_ref[...] = jnp.log(lb + (1.0 - lb) * jax.nn.sigmoid(fp))
    k_ref[...] = (1.0 - lb) * jax.nn.sigmoid(-fp)
    v_ref[...] = _dot(xb, w_ref[:, 2 * d:3 * d]).astype(BF16)
    g_ref[...] = _dot(xb, w_ref[:, 3 * d:4 * d])


def _hgrn_proj(x, w_in, lb):
    n, d = x.shape
    tm = _tile(n, ROW_TILE)
    outs = [jax.ShapeDtypeStruct((n, d), dt) for dt in (F32, F32, F32, BF16, F32)]
    return pl.pallas_call(
        _hgrn_proj_kernel,
        grid=(n // tm,),
        in_specs=[_row_spec(tm, d), _const_spec((d, 4 * d)), _const_spec((1, d))],
        out_specs=[_row_spec(tm, d)] * 5,
        out_shape=outs,
        compiler_params=_cparams("parallel"),
        name="hgrn_proj",
    )(x, w_in, lb)


def _cumsum_rows(x):
    c = x.shape[0]
    row = lax.broadcasted_iota(I32, x.shape, 0)
    sh = 1
    while sh < min(c, 8):
        x = x + jnp.where(row >= sh, pltpu.roll(x, sh, axis=0), 0.0)
        sh *= 2
    while sh < c:
        x = x + jnp.concatenate([jnp.zeros((sh, x.shape[1]), x.dtype), x[:c - sh, :]], axis=0)
        sh *= 2
    return x


def _level_decay(b, lf, m):
    c = b.shape[0]
    if m >= 4:
        parts = []
        for lo in range(0, c, 2 * m):
            parts.append(b[lo:lo + 2 * m, :] - b[lo + m - 1:lo + m, :])
        d = parts[0] if len(parts) == 1 else jnp.concatenate(parts, axis=0)
    else:
        row = lax.broadcasted_iota(I32, b.shape, 0)
        if m == 1:
            d = jnp.where((row & 1) == 0, 0.0, lf)
        else:
            ph = row & 3
            nxt = pltpu.roll(lf, c - 1, axis=0)
            prv = pltpu.roll(lf, 1, axis=0)
            d = jnp.where(ph == 0, nxt, jnp.where(ph == 1, 0.0, jnp.where(ph == 2, lf, lf + prv)))
    return jnp.exp(-jnp.abs(d))


def _hgrn_levels(chunk):
    levels = []
    m = chunk // 2
    while m >= 1:
        levels.append(m)
        m //= 2
    return levels


def _hgrn_masks(chunk):
    t = np.arange(chunk)[:, None]
    s = np.arange(chunk)[None, :]
    x = t ^ s
    masks = [((t > s) & (x >= m) & (x < 2 * m)) for m in _hgrn_levels(chunk)]
    masks.append(t == s)
    return jnp.asarray(np.stack(masks).astype(np.float32))


def _hgrn_rec_kernel(q_ref, k_ref, lf_ref, v_ref, g_ref, ng_ref, masks_ref, o_ref, st_ref, *, chunk, dh):
    hg = pl.program_id(1)
    heads = q_ref.shape[1] // dh
    levels = _hgrn_levels(chunk)

    @pl.when(pl.program_id(2) == 0)
    def _():
        st_ref[...] = jnp.zeros_like(st_ref)

    def head_chunk(sl, hh):
        ln = slice(hh * dh, (hh + 1) * dh)
        q, k, lf, v, g = q_ref[sl, ln], k_ref[sl, ln], lf_ref[sl, ln], v_ref[sl, ln], g_ref[sl, ln]
        b = _cumsum_rows(lf)
        s = masks_ref[len(levels)] * _dot_nt(q.astype(BF16), k.astype(BF16))
        for li, m in enumerate(levels):
            e = _level_decay(b, lf, m)
            s = s + masks_ref[li] * _dot_nt((q * e).astype(BF16), (k * e).astype(BF16))
        o = _dot(s.astype(BF16), v)
        st = st_ref[hh]
        b_last = b[chunk - 1:chunk, :]
        o = o + _dot_nt((q * jnp.exp(b)).astype(BF16), st.astype(BF16))
        kt = (k * jnp.exp(b_last - b)).astype(BF16)
        st_ref[hh] = st * jnp.exp(b_last) + _dot_tn(v, kt)
        ms = jnp.mean(o * o, axis=-1, keepdims=True)
        o = o * lax.rsqrt(ms + RMS_EPS) * ng_ref[pl.ds(hg * heads + hh, 1), :]
        o_ref[sl, ln] = (o * (g * jax.nn.sigmoid(g))).astype(o_ref.dtype)

    def body(c, carry):
        sl = pl.ds(pl.multiple_of(c * chunk, chunk), chunk)
        for hh in range(heads):
            head_chunk(sl, hh)
        return carry

    lax.fori_loop(0, q_ref.shape[0] // chunk, body, 0)


def _hgrn_rec(q, k, lf, v, g, norm_g, batch, seq):
    n, d = q.shape
    dh = d // HG_HEADS
    tt = _tile(seq, HG_TIME_TILE)
    chunk = _tile(tt, HG_CHUNK)
    nt = seq // tt
    masks = _hgrn_masks(chunk)
    blk = pl.BlockSpec((tt, HG_HEAD_GROUP * dh), lambda b, h, t: (b * nt + t, h))
    return pl.pallas_call(
        functools.partial(_hgrn_rec_kernel, chunk=chunk, dh=dh),
        grid=(batch, HG_HEADS // HG_HEAD_GROUP, nt),
        in_specs=[blk, blk, blk, blk, blk, _const_spec(norm_g.shape), _const_spec(masks.shape)],
        out_specs=blk,
        out_shape=jax.ShapeDtypeStruct((n, d), BF16),
        scratch_shapes=[pltpu.VMEM((HG_HEAD_GROUP, dh, dh), F32)],
        compiler_params=_cparams("parallel", "parallel", "arbitrary"),
        name="hgrn_rec",
    )(q, k, lf, v, g, norm_g, masks)


def _proj_ln_kernel(a_ref, w_ref, x_ref, g_ref, b_ref, o_ref, *, alpha):
    y = _dot(a_ref[...], w_ref[...])
    o_ref[...] = _layer_norm(alpha * x_ref[...] + y, g_ref[...], b_ref[...])


def _proj_ln(a, w, x, g, b, alpha):
    n, d = x.shape
    ka = a.shape[1]
    tm = _tile(n, ROW_TILE)
    return pl.pallas_call(
        functools.partial(_proj_ln_kernel, alpha=alpha),
        grid=(n // tm,),
        in_specs=[_row_spec(tm, ka), _const_spec((ka, d)), _row_spec(tm, d), _const_spec((1, d)),
                  _const_spec((1, d))],
        out_specs=_row_spec(tm, d),
        out_shape=jax.ShapeDtypeStruct((n, d), F32),
        compiler_params=_cparams("parallel"),
        name="proj_ln",
    )(a, w, x, g, b)


def _swiglu_step(xb, wg_ref, wu_ref, wd_ref):
    gate = _dot(xb, wg_ref[...])
    up = _dot(xb, wu_ref[...])
    act = (gate * jax.nn.sigmoid(gate) * up).astype(BF16)
    return _dot(act, wd_ref[...])


def _ffn_ln_kernel(x_ref, wg_ref, wu_ref, wd_ref, g_ref, b_ref, o_ref, acc_ref, *, alpha):
    f = pl.program_id(1)
    y = _swiglu_step(x_ref[...].astype(BF16), wg_ref, wu_ref, wd_ref)

    @pl.when(f == 0)
    def _():
        acc_ref[...] = y

    @pl.when(f > 0)
    def _():
        acc_ref[...] += y

    @pl.when(f == pl.num_programs(1) - 1)
    def _():
        o_ref[...] = _layer_norm(alpha * x_ref[...] + acc_ref[...], g_ref[...], b_ref[...])


def _ffn_ln(x, w_gate_up, w_down, g, b, alpha):
    n, d = x.shape
    ff = w_down.shape[0]
    tm = _tile(n, FFN_ROW_TILE)
    tf = _tile(ff, FFN_F_TILE)
    nf = ff // tf
    return pl.pallas_call(
        functools.partial(_ffn_ln_kernel, alpha=alpha),
        grid=(n // tm, nf),
        in_specs=[pl.BlockSpec((tm, d), lambda i, f: (i, 0)),
                  pl.BlockSpec((d, tf), lambda i, f: (0, f)),
                  pl.BlockSpec((d, tf), lambda i, f: (0, f + nf)),
                  pl.BlockSpec((tf, d), lambda i, f: (f, 0)),
                  _const_spec((1, d)), _const_spec((1, d))],
        out_specs=pl.BlockSpec((tm, d), lambda i, f: (i, 0)),
        out_shape=jax.ShapeDtypeStruct((n, d), F32),
        scratch_shapes=[pltpu.VMEM((tm, d), F32)],
        compiler_params=_cparams("parallel", "arbitrary"),
        name="ffn_ln",
    )(x, w_gate_up, w_gate_up, w_down, g, b)


def _rows_from_tiles(t_ref):
    return jnp.concatenate([t_ref[:, c, :] for c in range(t_ref.shape[1])], axis=1)


def _rows_to_tiles(t_ref, y):
    for c in range(t_ref.shape[1]):
        t_ref[:, c, :] = y[:, c * LANES:(c + 1) * LANES]


def _moe_ffn_kernel(te_ref, nt_ref, xs_hbm, wg_ref, wu_ref, wd_ref, ys_hbm,
                    xrow_ref, xb_ref, acc_ref, ybuf_ref, in_sem, out_sem):
    i, f = pl.program_id(0), pl.program_id(1)
    n_live = nt_ref[0]
    live = i < n_live
    last = f == pl.num_programs(1) - 1
    tm, d = acc_ref.shape

    def fetch(tile, slot):
        return [pltpu.make_async_copy(xs_hbm.at[pl.ds(tile * tm, tm), c, :],
                                      xrow_ref.at[slot, :, pl.ds(c * LANES, LANES)], in_sem.at[slot])
                for c in range(d // LANES)]

    def write(tile):
        return [pltpu.make_async_copy(ybuf_ref.at[:, pl.ds(c * LANES, LANES)],
                                      ys_hbm.at[pl.ds(tile * tm, tm), c, :], out_sem)
                for c in range(d // LANES)]

    @pl.when(live & (f == 0))
    def _():
        @pl.when(i == 0)
        def _():
            for cp in fetch(0, 0):
                cp.start()

        @pl.when(i + 1 < n_live)
        def _():
            for cp in fetch(i + 1, (i + 1) % 2):
                cp.start()

        for cp in fetch(i, i % 2):
            cp.wait()
        xb_ref[...] = xrow_ref[i % 2].astype(BF16)

    def flush(y_rows):
        @pl.when(i > 0)
        def _():
            for cp in write(i - 1):
                cp.wait()

        ybuf_ref[...] = y_rows()
        for cp in write(i):
            cp.start()

    @pl.when(live)
    def _():
        y = _swiglu_step(xb_ref[...], wg_ref, wu_ref, wd_ref)

        @pl.when(f == 0)
        def _():
            acc_ref[...] = y

        @pl.when((f > 0) & jnp.logical_not(last))
        def _():
            acc_ref[...] += y

        @pl.when(last)
        def _():
            flush(lambda: acc_ref[...] + y)

    @pl.when(jnp.logical_not(live) & last)
    def _():
        flush(lambda: jnp.zeros(ybuf_ref.shape, F32))

    @pl.when(last & (i == pl.num_programs(0) - 1))
    def _():
        for cp in write(i):
            cp.wait()


def _moe_ffn(tile_expert, n_live, xs, w_gate_up, w_down, tm):
    p, sub, lanes = xs.shape
    d = sub * lanes
    ff = w_down.shape[1]
    assert p % tm == 0
    tf = _tile(ff, FFN_F_TILE)
    nf = ff // tf
    assert nf > 1
    grid_spec = pltpu.PrefetchScalarGridSpec(
        num_scalar_prefetch=2,
        grid=(p // tm, nf),
        in_specs=[pl.BlockSpec(memory_space=pl.ANY),
                  pl.BlockSpec((None, d, tf), lambda i, f, te, nt: (te[i], 0, f)),
                  pl.BlockSpec((None, d, tf), lambda i, f, te, nt: (te[i], 0, f + nf)),
                  pl.BlockSpec((None, tf, d), lambda i, f, te, nt: (te[i], f, 0))],
        out_specs=pl.BlockSpec(memory_space=pl.ANY),
        scratch_shapes=[pltpu.VMEM((2, tm, d), F32), pltpu.VMEM((tm, d), BF16), pltpu.VMEM((tm, d), F32),
                        pltpu.VMEM((tm, d), F32), pltpu.SemaphoreType.DMA((2,)), pltpu.SemaphoreType.DMA(())],
    )
    return pl.pallas_call(
        _moe_ffn_kernel,
        grid_spec=grid_spec,
        out_shape=jax.ShapeDtypeStruct(xs.shape, F32),
        compiler_params=_cparams("arbitrary", "arbitrary"),
        name="moe_ffn",
    )(tile_expert, n_live, xs, w_gate_up, w_gate_up, w_down)


def _router_kernel(x_ref, w_ref, tril_ref, gate_ref, idx_ref, cnt_ref, carry_ref):
    @pl.when(pl.program_id(0) == 0)
    def _():
        carry_ref[...] = jnp.zeros_like(carry_ref)

    logits = jnp.dot(x_ref[...], w_ref[...], preferred_element_type=F32, precision=lax.Precision.HIGHEST)
    lane = lax.broadcasted_iota(I32, logits.shape, 1)
    l1 = jnp.where(lane < N_EXPERTS, logits, -jnp.inf)
    m1 = jnp.max(l1, axis=-1, keepdims=True)
    i1 = jnp.min(jnp.where(l1 == m1, lane, LANES), axis=-1, keepdims=True)
    l2 = jnp.where(lane == i1, -jnp.inf, l1)
    m2 = jnp.max(l2, axis=-1, keepdims=True)
    i2 = jnp.min(jnp.where(l2 == m2, lane, LANES), axis=-1, keepdims=True)
    e = jnp.exp(m2 - m1)
    den = 1.0 + e
    gate_ref[...] = jnp.where(lane == 0, 1.0 / den, jnp.where(lane == 1, e / den, 0.0))
    oh1 = jnp.where(lane == i1, 1.0, 0.0)
    oh2 = jnp.where(lane == i2, 1.0, 0.0)
    oh = oh1 + oh2
    before = _dot(tril_ref[...], oh.astype(BF16)) + carry_ref[...]
    r1 = jnp.sum(oh1 * before, axis=-1, keepdims=True).astype(I32)
    r2 = jnp.sum(oh2 * before, axis=-1, keepdims=True).astype(I32)
    carry_ref[...] += jnp.sum(oh, axis=0, keepdims=True)
    cnt_ref[...] = carry_ref[...]
    idx_ref[...] = jnp.where(lane == 0, i1, jnp.where(lane == 1, i2, jnp.where(lane == 2, r1, jnp.where(
        lane == 3, r2, 0))))


def _router(x, w_router_pad):
    n, d = x.shape
    tm = _tile(n, ROW_TILE)
    tril = jnp.asarray(np.tril(np.ones((tm, tm), np.float32), -1), BF16)
    return pl.pallas_call(
        _router_kernel,
        grid=(n // tm,),
        in_specs=[_row_spec(tm, d), _const_spec((d, LANES)), _const_spec((tm, tm))],
        out_specs=[_row_spec(tm, LANES), _row_spec(tm, LANES), _const_spec((1, LANES))],
        out_shape=[jax.ShapeDtypeStruct((n, LANES), F32), jax.ShapeDtypeStruct((n, LANES), I32),
                   jax.ShapeDtypeStruct((1, LANES), F32)],
        scratch_shapes=[pltpu.VMEM((1, LANES), F32)],
        compiler_params=_cparams("arbitrary"),
        name="router",
    )(x, w_router_pad, tril)


def _moe_dispatch_kernel(p1_ref, p2_ref, x_ref, xs_init_hbm, xs_hbm, stage_ref, sem):
    del xs_init_hbm
    td = x_ref.shape[0]
    base = pl.program_id(0) * td
    _rows_to_tiles(stage_ref, x_ref[...])

    def copies(r, row1, row2):
        return (pltpu.make_async_copy(stage_ref.at[r], xs_hbm.at[row1], sem),
                pltpu.make_async_copy(stage_ref.at[r], xs_hbm.at[row2], sem))

    def issue(r, c):
        for cp in copies(r, p1_ref[base + r], p2_ref[base + r]):
            cp.start()
        return c

    def drain(r, c):
        for cp in copies(r, 0, 0):
            cp.wait()
        return c

    lax.fori_loop(0, td, issue, 0, unroll=DMA_LOOP_UNROLL)
    lax.fori_loop(0, td, drain, 0, unroll=DMA_LOOP_UNROLL)


def _moe_dispatch(p1, p2, x, p_rows):
    n, d = x.shape
    sub = d // LANES
    td = _tile(n, GATHER_TILE)
    xs_init = jnp.zeros((p_rows, sub, LANES), F32)
    grid_spec = pltpu.PrefetchScalarGridSpec(
        num_scalar_prefetch=2,
        grid=(n // td,),
        in_specs=[pl.BlockSpec((td, d), lambda i, a, c: (i, 0)), pl.BlockSpec(memory_space=pl.ANY)],
        out_specs=pl.BlockSpec(memory_space=pl.ANY),
        scratch_shapes=[pltpu.VMEM((td, sub, LANES), F32), pltpu.SemaphoreType.DMA(())],
    )
    return pl.pallas_call(
        _moe_dispatch_kernel,
        grid_spec=grid_spec,
        out_shape=jax.ShapeDtypeStruct(xs_init.shape, F32),
        input_output_aliases={3: 0},
        compiler_params=_cparams("arbitrary"),
        name="moe_dispatch",
    )(p1, p2, x, xs_init)


def _moe_combine_ln_kernel(p1_ref, p2_ref, x_ref, gate_ref, ys_hbm, g_ref, b_ref, o_ref, buf_ref, sem, *, alpha):
    i, nsteps = pl.program_id(0), pl.num_programs(0)
    tc = o_ref.shape[0]

    def copies(slot, r, row1, row2):
        return (pltpu.make_async_copy(ys_hbm.at[row1], buf_ref.at[slot, 0, r], sem.at[slot]),
                pltpu.make_async_copy(ys_hbm.at[row2], buf_ref.at[slot, 1, r], sem.at[slot]))

    def issue(step, slot):
        def body(r, c):
            for cp in copies(slot, r, p1_ref[step * tc + r], p2_ref[step * tc + r]):
                cp.start()
            return c

        lax.fori_loop(0, tc, body, 0, unroll=DMA_LOOP_UNROLL)

    @pl.when(i == 0)
    def _():
        issue(0, 0)

    @pl.when(i + 1 < nsteps)
    def _():
        issue(i + 1, (i + 1) % 2)

    slot = i % 2

    def drain(r, c):
        for cp in copies(slot, r, 0, 0):
            cp.wait()
        return c

    lax.fori_loop(0, tc, drain, 0, unroll=DMA_LOOP_UNROLL)
    w = gate_ref[...]
    y = w[:, 0:1] * _rows_from_tiles(buf_ref.at[slot, 0]) + w[:, 1:2] * _rows_from_tiles(buf_ref.at[slot, 1])
    o_ref[...] = _layer_norm(alpha * x_ref[...] + y, g_ref[...], b_ref[...])


def _moe_combine_ln(p1, p2, x, gate, ys, g, b, alpha):
    n, d = x.shape
    sub = d // LANES
    tc = _tile(n, COMBINE_TILE)
    grid_spec = pltpu.PrefetchScalarGridSpec(
        num_scalar_prefetch=2,
        grid=(n // tc,),
        in_specs=[pl.BlockSpec((tc, d), lambda i, a, c: (i, 0)),
                  pl.BlockSpec((tc, LANES), lambda i, a, c: (i, 0)),
                  pl.BlockSpec(memory_space=pl.ANY),
                  pl.BlockSpec((1, d), lambda i, a, c: (0, 0)),
                  pl.BlockSpec((1, d), lambda i, a, c: (0, 0))],
        out_specs=pl.BlockSpec((tc, d), lambda i, a, c: (i, 0)),
        scratch_shapes=[pltpu.VMEM((2, 2, tc, sub, LANES), F32), pltpu.SemaphoreType.DMA((2,))],
    )
    return pl.pallas_call(
        functools.partial(_moe_combine_ln_kernel, alpha=alpha),
        grid_spec=grid_spec,
        out_shape=jax.ShapeDtypeStruct((n, d), F32),
        compiler_params=_cparams("arbitrary"),
        name="moe_combine_ln",
    )(p1, p2, x, gate, ys, g, b)


def _moe_layer(x, w_router_pad, w_gate_up, w_down, g, b, alpha):
    n = x.shape[0]
    gate, idx, cnt = _router(x, w_router_pad)
    tm = min(MOE_ROW_TILE, 2 * n)
    p_rows = 2 * n + N_EXPERTS * tm
    counts = cnt[0, :N_EXPERTS].astype(I32)
    padded = ((counts + tm - 1) // tm) * tm
    pends = jnp.cumsum(padded)
    pstarts = pends - padded
    p1 = pstarts[idx[:, 0]] + idx[:, 2]
    p2 = pstarts[idx[:, 1]] + idx[:, 3]
    tile_start = jnp.arange(p_rows // tm, dtype=I32) * tm
    tile_expert = jnp.minimum(jnp.sum(tile_start[:, None] >= pends[None, :], axis=1), N_EXPERTS - 1).astype(I32)
    n_live = (pends[-1] // tm).astype(I32).reshape(1)
    xs = _moe_dispatch(p1, p2, x, p_rows)
    ys = _moe_ffn(tile_expert, n_live, xs, w_gate_up, w_down, tm)
    return _moe_combine_ln(p1, p2, x, gate, ys, g, b, alpha)


def _rms_norm(x, g):
    ms = jnp.mean(x * x, axis=-1, keepdims=True)
    return x * lax.rsqrt(ms + RMS_EPS) * g


def _dsa_proj_kernel(x_ref, win_ref, qg_ref, kvg_ref, kig_ref, kib_ref, wuq_ref, wuk_ref, wqi_ref,
                     qlat_ref, qidx_ref, ckv_ref, misc_ref, *, w_scale):
    xb = x_ref[...].astype(BF16)
    qr, kr = DSA_Q_RANK, DSA_KV_RANK
    cq = _rms_norm(_dot(xb, win_ref[:, 0:qr]), qg_ref[...]).astype(BF16)
    ckv_ref[...] = _rms_norm(_dot(xb, win_ref[:, qr:qr + kr]), kvg_ref[...]).astype(ckv_ref.dtype)
    misc = _dot(xb, win_ref[:, qr + kr:qr + kr + LANES])
    lane = lax.broadcasted_iota(I32, misc.shape, 1)
    is_key = lane < IDX_DIM
    mu = jnp.sum(jnp.where(is_key, misc, 0.0), axis=-1, keepdims=True) * (1.0 / IDX_DIM)
    kc = jnp.where(is_key, misc - mu, 0.0)
    var = jnp.sum(kc * kc, axis=-1, keepdims=True) * (1.0 / IDX_DIM)
    kn = kc * lax.rsqrt(var + LN_EPS) * kig_ref[...] + kib_ref[...]
    misc_ref[...] = jnp.where(is_key, kn, misc * w_scale)
    q = _dot(cq, wuq_ref[...])
    hd = DSA_HEAD_DIM
    for h in range(DSA_HEADS):
        qidx_ref[h] = _dot(cq, wqi_ref[:, h * LANES:(h + 1) * LANES]).astype(qidx_ref.dtype)
        qlat_ref[h] = _dot(q[:, h * hd:(h + 1) * hd].astype(BF16), wuk_ref[h]).astype(qlat_ref.dtype)


def _head_rows_spec(t, wd):
    return pl.BlockSpec((DSA_HEADS, t, wd), lambda i: (0, i, 0))


def _dsa_proj(x, w_in_pad, q_g, kv_g, ki_g, ki_b, w_uq, w_uk, w_qidx_pad):
    n, d = x.shape
    tm = _tile(n, ROW_TILE)
    w_scale = IDX_HEADS ** -0.5 * IDX_DIM ** -0.5
    outs = [jax.ShapeDtypeStruct((DSA_HEADS, n, DSA_KV_RANK), BF16),
            jax.ShapeDtypeStruct((IDX_HEADS, n, LANES), BF16),
            jax.ShapeDtypeStruct((n, DSA_KV_RANK), BF16),
            jax.ShapeDtypeStruct((n, LANES), F32)]
    ins = [x, w_in_pad, q_g, kv_g, ki_g, ki_b, w_uq, w_uk, w_qidx_pad]
    return pl.pallas_call(
        functools.partial(_dsa_proj_kernel, w_scale=w_scale),
        grid=(n // tm,),
        in_specs=[_row_spec(tm, d)] + [_const_spec(a.shape) for a in ins[1:]],
        out_specs=[_head_rows_spec(tm, DSA_KV_RANK), _head_rows_spec(tm, LANES), _row_spec(tm, DSA_KV_RANK),
                   _row_spec(tm, LANES)],
        out_shape=outs,
        compiler_params=_cparams("parallel"),
        name="dsa_proj",
    )(*ins)


def _order_key(x):
    k = pltpu.bitcast(x, I32)
    k = k ^ ((k >> 31) & 0x7FFFFFFF)
    return jnp.where(x == 0.0, 0, k)


def _dsa_attn_kernel(qidx_ref, wt_ref, qlat_ref, kidx_ref, ckv_t_ref, ckv_ref, o_ref,
                     key_ref, plane_ref, bias_ref, s_ref, p_ref, m_ref, l_ref, alpha_ref, acc_ref,
                     *, topk, scale, tk, rc, rg):
    nh, tq, kr = qlat_ref.shape
    seq = ckv_ref.shape[0]
    q0 = pl.program_id(1) * tq
    n_tiles = (q0 + tq + tk - 1) // tk
    int_min = jnp.iinfo(jnp.int32).min
    wr = tk // WORD_BITS
    nbits = max(1, (seq - 1).bit_length())
    w_t = wt_ref[...]
    q_lane = q0 + lax.broadcasted_iota(I32, (tk, tq), 1)
    key_row = lax.broadcasted_iota(I32, (tk, tq), 0)
    sub = lax.broadcasted_iota(I32, (SUBLANES, tq), 0)

    def key_tile(kb):
        return pl.ds(pl.multiple_of(kb * tk, tk), tk)

    q_idx_all = qidx_ref[...].reshape(nh * tq, LANES)

    def score_tile(kb, c):
        r = _dot_nt(kidx_ref[key_tile(kb), :], q_idx_all)
        score = jnp.zeros((tk, tq), F32)
        for h in range(nh):
            score = score + jnp.maximum(r[:, h * tq:(h + 1) * tq], 0.0) * w_t[h:h + 1, :]
        score = jnp.where(kb * tk + key_row <= q_lane, score, NEG_BIG)
        key = _order_key(score)
        key_ref[key_tile(kb), :] = key
        u = key ^ int_min
        slabs = [u[j * wr:(j + 1) * wr, :] for j in range(WORD_BITS)]
        step, mask = WORD_BITS // 2, (1 << (WORD_BITS // 2)) - 1
        while step:
            for lo in range(WORD_BITS):
                if lo & step == 0:
                    hi = lo + step
                    d = (slabs[lo] ^ lax.shift_right_logical(slabs[hi], jnp.full_like(slabs[hi], step))) & mask
                    slabs[lo] = slabs[lo] ^ d
                    slabs[hi] = slabs[hi] ^ (d << step)
            step //= 2
            if step:
                mask = mask ^ (mask << step)
        for b in range(WORD_BITS):
            plane_ref[WORD_BITS - 1 - b, pl.ds(pl.multiple_of(kb * wr, wr), wr), :] = slabs[b]
        return c

    plane_ref[...] = jnp.zeros(plane_ref.shape, I32)
    lax.fori_loop(0, n_tiles, score_tile, 0)

    def count_keys(pred):
        n_acc = 4

        def tile(kb, accs):
            blk = key_ref[key_tile(kb), :]
            accs = list(accs)
            for s8 in range(tk // SUBLANES):
                hit = pred(blk[s8 * SUBLANES:(s8 + 1) * SUBLANES, :], kb * tk + s8 * SUBLANES)
                accs[s8 % n_acc] = accs[s8 % n_acc] + jnp.where(hit, 1.0, 0.0)
            return tuple(accs)

        accs = lax.fori_loop(0, n_tiles, tile, tuple(jnp.zeros((SUBLANES, tq), F32) for _ in range(n_acc)))
        return jnp.sum((accs[0] + accs[1]) + (accs[2] + accs[3]), axis=0, keepdims=True)

    def rep(row):
        return jnp.broadcast_to(row, (SUBLANES, tq))

    def bit_step(it, carry):
        cands, above, t = carry
        sig = WORD_BITS - 1 - it
        ones = cands & plane_ref[sig]
        n_ones = jnp.sum(lax.population_count(ones), axis=0, keepdims=True)
        take = above + n_ones >= topk
        cands = jnp.where(take, ones, cands ^ ones)
        above = jnp.where(take, above, above + n_ones)
        t = t | jnp.where(take, lax.shift_left(jnp.int32(1), sig), 0)
        return cands, above, t

    cands, above, t_u = lax.fori_loop(
        0, WORD_BITS, bit_step,
        (jnp.full((seq // WORD_BITS, tq), -1, I32), jnp.zeros((1, tq), I32), jnp.zeros((1, tq), I32)))
    t_all = t_u ^ int_min
    t_b = rep(t_all)
    need = (topk - above).astype(F32)
    extra = jnp.sum(lax.population_count(cands), axis=0, keepdims=True).astype(F32) - need

    def last_kept_tie():
        def index_step(it, j):
            cand = j + lax.shift_left(jnp.int32(1), nbits - 1 - it)
            cand_b = rep(cand)
            below = count_keys(lambda k, r0: (k == t_b) & (r0 + sub < cand_b))
            return jnp.where(below < need, cand, j)

        return lax.fori_loop(0, nbits, index_step, jnp.zeros((1, tq), I32))

    j_all = lax.cond(jnp.max(extra) > 0.0, last_kept_tie, lambda: jnp.full((1, tq), seq, I32))
    q_lat_all = qlat_ref[...].reshape(nh * tq, kr)
    m_ref[...] = jnp.full(m_ref.shape, NEG_BIG, F32)
    l_ref[...] = jnp.zeros(l_ref.shape, F32)
    acc_ref[...] = jnp.zeros(acc_ref.shape, F32)
    c2 = scale * LOG2_E

    def attn_tile(kb, c):
        key = key_ref[key_tile(kb), :]
        krow = kb * tk + key_row
        sel = ((key > t_all) | ((key == t_all) & (krow <= j_all))) & (krow <= q_lane)
        bias_ref[...] = jnp.where(sel, 0.0, NEG_BIG).T
        ckv_t = ckv_t_ref[:, key_tile(kb)]
        ckv = ckv_ref[key_tile(kb), :]
        for g0 in range(0, nh * tq, rg):
            grp = slice(g0, g0 + rg)
            s_ref[grp, :] = _dot(q_lat_all[grp, :], ckv_t)
            for r0 in range(g0, g0 + rg, rc):
                rows = slice(r0, r0 + rc)
                s = s_ref[rows, :] + bias_ref[r0 % tq:r0 % tq + rc, :]
                m_prev = m_ref[rows, :]
                m_new = jnp.maximum(m_prev, jnp.max(s, axis=-1, keepdims=True))
                alpha = jnp.exp2((m_prev - m_new) * c2)
                p_lanes = None
                for g in range(tk // LANES):
                    lanes = slice(g * LANES, (g + 1) * LANES)
                    p = jnp.exp2((s[:, lanes] - m_new) * c2)
                    p_ref[rows, lanes] = p.astype(BF16)
                    p_lanes = p if p_lanes is None else p_lanes + p
                l_ref[rows, :] = alpha * l_ref[rows, :] + p_lanes
                m_ref[rows, :] = m_new
                alpha_ref[rows, :] = alpha
            pv = _dot(p_ref[grp, :], ckv)
            a = alpha_ref[grp, :]
            for g in range(kr // LANES):
                lanes = slice(g * LANES, (g + 1) * LANES)
                acc_ref[grp, lanes] = a * acc_ref[grp, lanes] + pv[:, lanes]
        return c

    lax.fori_loop(0, n_tiles, attn_tile, 0)
    den = jnp.sum(l_ref[...], axis=-1, keepdims=True)
    o_ref[...] = (acc_ref[...] / den).reshape(nh, tq, kr).astype(o_ref.dtype)


def _dsa_attn(q_idx, w_t, q_lat, kidx, ckv_t, ckv, batch, seq):
    nh, n, kr = q_lat.shape
    tq = _tile(seq, DSA_Q_TILE)
    tk = _tile(seq, DSA_KEY_TILE)
    rc = _tile(tq, DSA_SOFTMAX_ROWS)
    nq = seq // tq
    topk = min(IDX_TOPK_MAX, seq // 4)
    heads = lambda wd: pl.BlockSpec((nh, tq, wd), lambda b, i: (0, b * nq + i, 0))
    stat = pltpu.VMEM((nh * tq, LANES), F32)
    return pl.pallas_call(
        functools.partial(_dsa_attn_kernel, topk=topk, scale=DSA_HEAD_DIM ** -0.5, tk=tk, rc=rc,
                          rg=_tile(nh * tq, DSA_GROUP_ROWS)),
        grid=(batch, nq),
        in_specs=[heads(LANES), pl.BlockSpec((None, IDX_HEADS, tq), lambda b, i: (b, 0, i)), heads(kr),
                  pl.BlockSpec((seq, LANES), lambda b, i: (b, 0)),
                  pl.BlockSpec((None, kr, seq), lambda b, i: (b, 0, 0)),
                  pl.BlockSpec((seq, kr), lambda b, i: (b, 0))],
        out_specs=heads(kr),
        out_shape=jax.ShapeDtypeStruct((nh, n, kr), BF16),
        scratch_shapes=[pltpu.VMEM((seq, tq), I32), pltpu.VMEM((WORD_BITS, seq // WORD_BITS, tq), I32),
                        pltpu.VMEM((tq, tk), F32),
                        pltpu.VMEM((nh * tq, tk), F32), pltpu.VMEM((nh * tq, tk), BF16),
                        stat, stat, stat, pltpu.VMEM((nh * tq, kr), F32)],
        compiler_params=_cparams("parallel", "parallel"),
        name="dsa_attn",
    )(q_idx, w_t, q_lat, kidx, ckv_t, ckv)


def _dsa_out_ln_kernel(olat_ref, wuv_ref, wout_ref, x_ref, g_ref, b_ref, o_ref, *, alpha):
    hd = DSA_HEAD_DIM
    y = None
    for h in range(DSA_HEADS):
        oh = _dot(olat_ref[h], wuv_ref[h]).astype(BF16)
        yh = _dot(oh, wout_ref[h * hd:(h + 1) * hd, :])
        y = yh if y is None else y + yh
    o_ref[...] = _layer_norm(alpha * x_ref[...] + y, g_ref[...], b_ref[...])


def _dsa_out_ln(o_lat, w_uv, w_out, x, g, b, alpha):
    n, d = x.shape
    tm = _tile(n, ROW_TILE)
    return pl.pallas_call(
        functools.partial(_dsa_out_ln_kernel, alpha=alpha),
        grid=(n // tm,),
        in_specs=[_head_rows_spec(tm, o_lat.shape[2]), _const_spec(w_uv.shape), _const_spec(w_out.shape),
                  _row_spec(tm, d), _const_spec((1, d)), _const_spec((1, d))],
        out_specs=_row_spec(tm, d),
        out_shape=jax.ShapeDtypeStruct((n, d), F32),
        compiler_params=_cparams("parallel"),
        name="dsa_out_ln",
    )(o_lat, w_uv, w_out, x, g, b)


def _dsa_layer(x, batch, seq, w_in, q_g, kv_g, w_uq, w_uk, w_uv, w_qidx, ki_g, ki_b, w_out, g, b, alpha):
    d = x.shape[1]
    qk = DSA_Q_RANK + DSA_KV_RANK
    tail = w_in.shape[1] - qk
    w_in_pad = jnp.pad(w_in, ((0, 0), (0, LANES - tail))).astype(BF16)
    w_qidx_pad = jnp.pad(w_qidx.reshape(DSA_Q_RANK, IDX_HEADS, IDX_DIM),
                         ((0, 0), (0, 0), (0, LANES - IDX_DIM))).reshape(DSA_Q_RANK, IDX_HEADS * LANES)
    pad_lane = lambda v: jnp.pad(v, (0, LANES - v.shape[0])).reshape(1, LANES)
    q_lat, q_idx, ckv, misc = _dsa_proj(
        x, w_in_pad, q_g.reshape(1, -1), kv_g.reshape(1, -1), pad_lane(ki_g), pad_lane(ki_b),
        w_uq.astype(BF16), w_uk.astype(BF16), w_qidx_pad.astype(BF16))
    lane = jnp.arange(LANES)
    kidx = jnp.where(lane < IDX_DIM, misc, 0.0).astype(BF16)
    w_t = misc[:, IDX_DIM:IDX_DIM + IDX_HEADS].reshape(batch, seq, IDX_HEADS).transpose(0, 2, 1)
    ckv_t = ckv.reshape(batch, seq, DSA_KV_RANK).transpose(0, 2, 1)
    o_lat = _dsa_attn(q_idx, w_t, q_lat, kidx, ckv_t, ckv, batch, seq)
    return _dsa_out_ln(o_lat, w_uv.astype(BF16), w_out.astype(BF16), x, g, b, alpha)


def kernel(x, ln_g, ln_b, hg_w_in, hg_lower_bounds, hg_norm_g, hg_w_out, dsa_w_in, dsa_q_norm_g, dsa_kv_norm_g, dsa_w_uq, dsa_w_uk, dsa_w_uv, dsa_w_qidx, dsa_kidx_norm_g, dsa_kidx_norm_b, dsa_w_out, ffn_w_gate_up, ffn_w_down, moe_w_router, moe_w_gate_up, moe_w_down):
    batch, seq, d = x.shape
    depth = ln_g.shape[0]
    alpha = (2 * depth) ** 0.25
    lb_all = jax.nn.softmax(hg_lower_bounds.astype(F32), axis=0)
    lb_all = jnp.cumsum(lb_all, axis=0) - lb_all[0]
    h = x.reshape(batch * seq, d)
    row = lambda v: v.reshape(1, d)
    for layer in range(depth):
        j = layer // 2
        g0, b0, g1, b1 = row(ln_g[layer, 0]), row(ln_b[layer, 0]), row(ln_g[layer, 1]), row(ln_b[layer, 1])
        if layer % 2 == 0:
            q, k, lf, v, g = _hgrn_proj(h, hg_w_in[j].astype(BF16), row(lb_all[j]))
            o = _hgrn_rec(q, k, lf, v, g, hg_norm_g[j], batch, seq)
            h = _proj_ln(o, hg_w_out[j].astype(BF16), h, g0, b0, alpha)
            h = _ffn_ln(h, ffn_w_gate_up[j].astype(BF16), ffn_w_down[j].astype(BF16), g1, b1, alpha)
        else:
            h = _dsa_layer(h, batch, seq, dsa_w_in[j], dsa_q_norm_g[j], dsa_kv_norm_g[j], dsa_w_uq[j],
                           dsa_w_uk[j], dsa_w_uv[j], dsa_w_qidx[j], dsa_kidx_norm_g[j], dsa_kidx_norm_b[j],
                           dsa_w_out[j], g0, b0, alpha)
            w_router_pad = jnp.pad(moe_w_router[j], ((0, 0), (0, LANES - N_EXPERTS)))
            h = _moe_layer(h, w_router_pad, moe_w_gate_up[j].astype(BF16), moe_w_down[j].astype(BF16),
                           g1, b1, alpha)
    return h.reshape(batch, seq, d)
```

```python
import functools

import numpy as np
import jax
import jax.numpy as jnp
from jax import lax
from jax.experimental import pallas as pl
from jax.experimental.pallas import tpu as pltpu

F32, BF16, I32 = jnp.float32, jnp.bfloat16, jnp.int32

HG_HEADS = 8
DSA_HEADS = 8
DSA_HEAD_DIM = 128
DSA_Q_RANK = 256
DSA_KV_RANK = 256
IDX_HEADS = 8
IDX_DIM = 64
IDX_TOPK_MAX = 256
N_EXPERTS = 8
LN_EPS = 1e-5
RMS_EPS = 1e-6
NEG_BIG = -1e30

LANES = 128
SUBLANES = 8
WORD_BITS = 32
V7X_VMEM_BYTES = 64 * 1024 * 1024
VMEM_LIMIT = V7X_VMEM_BYTES * 7 // 8

ROW_TILE = 512
FFN_ROW_TILE = 1024
FFN_F_TILE = 512
MOE_ROW_TILE = 512
HG_TIME_TILE = 512
HG_CHUNK = 128
HG_HEAD_GROUP = 8
DSA_Q_TILE = 256
DSA_KEY_TILE = 512
DSA_SOFTMAX_ROWS = 128
DSA_GROUP_ROWS = 512
LOG2_E = 1.4426950408889634
GATHER_TILE = 512
COMBINE_TILE = 256
DMA_LOOP_UNROLL = 8


def _cparams(*sem):
    return pltpu.CompilerParams(dimension_semantics=sem, vmem_limit_bytes=VMEM_LIMIT)


def _tile(n, t):
    t = min(n, t)
    assert n % t == 0, (n, t)
    return t


def _dot(a, b):
    return jnp.dot(a, b, preferred_element_type=F32)


def _dot_nt(a, b):
    return lax.dot_general(a, b, (((1,), (1,)), ((), ())), preferred_element_type=F32)


def _dot_tn(a, b):
    return lax.dot_general(a, b, (((0,), (0,)), ((), ())), preferred_element_type=F32)


def _layer_norm(y, g, b):
    mu = jnp.mean(y, axis=-1, keepdims=True)
    yc = y - mu
    var = jnp.mean(yc * yc, axis=-1, keepdims=True)
    return yc * lax.rsqrt(var + LN_EPS) * g + b


def _row_spec(t, d):
    return pl.BlockSpec((t, d), lambda i: (i, 0))


def _const_spec(shape):
    return pl.BlockSpec(shape, lambda *_: (0,) * len(shape))


def _hgrn_proj_kernel(x_ref, w_ref, lb_ref, q_ref, k_ref, lf_ref, v_ref, g_ref):
    d = x_ref.shape[1]
    xb = x_ref[...].astype(BF16)
    lb = lb_ref[...]
    q_ref[...] = _dot(xb, w_ref[:, 0:d])
    fp = _dot(xb, w_ref[:, d:2 * d])
    lf_ref[...] = jnp.log(lb + (1.0 - lb) * jax.nn.sigmoid(fp))
    k_ref[...] = (1.0 - lb) * jax.nn.sigmoid(-fp)
    v_ref[...] = _dot(xb, w_ref[:, 2 * d:3 * d]).astype(BF16)
    g_ref[...] = _dot(xb, w_ref[:, 3 * d:4 * d])


def _hgrn_proj(x, w_in, lb):
    n, d = x.shape
    tm = _tile(n, ROW_TILE)
    outs = [jax.ShapeDtypeStruct((n, d), dt) for dt in (F32, F32, F32, BF16, F32)]
    return pl.pallas_call(
        _hgrn_proj_kernel,
        grid=(n // tm,),
        in_specs=[_row_spec(tm, d), _const_spec((d, 4 * d)), _const_spec((1, d))],
        out_specs=[_row_spec(tm, d)] * 5,
        out_shape=outs,
        compiler_params=_cparams("parallel"),
        name="hgrn_proj",
    )(x, w_in, lb)


def _cumsum_rows(x):
    c = x.shape[0]
    row = lax.broadcasted_iota(I32, x.shape, 0)
    sh = 1
    while sh < min(c, 8):
        x = x + jnp.where(row >= sh, pltpu.roll(x, sh, axis=0), 0.0)
        sh *= 2
    while sh < c:
        x = x + jnp.concatenate([jnp.zeros((sh, x.shape[1]), x.dtype), x[:c - sh, :]], axis=0)
        sh *= 2
    return x


def _level_decay(b, lf, m):
    c = b.shape[0]
    if m >= 4:
        parts = []
        for lo in range(0, c, 2 * m):
            parts.append(b[lo:lo + 2 * m, :] - b[lo + m - 1:lo + m, :])
        d = parts[0] if len(parts) == 1 else jnp.concatenate(parts, axis=0)
    else:
        row = lax.broadcasted_iota(I32, b.shape, 0)
        if m == 1:
            d = jnp.where((row & 1) == 0, 0.0, lf)
        else:
            ph = row & 3
            nxt = pltpu.roll(lf, c - 1, axis=0)
            prv = pltpu.roll(lf, 1, axis=0)
            d = jnp.where(ph == 0, nxt, jnp.where(ph == 1, 0.0, jnp.where(ph == 2, lf, lf + prv)))
    return jnp.exp(-jnp.abs(d))


def _hgrn_levels(chunk):
    levels = []
    m = chunk // 2
    while m >= 1:
        levels.append(m)
        m //= 2
    return levels


def _hgrn_masks(chunk):
    t = np.arange(chunk)[:, None]
    s = np.arange(chunk)[None, :]
    x = t ^ s
    masks = [((t > s) & (x >= m) & (x < 2 * m)) for m in _hgrn_levels(chunk)]
    masks.append(t == s)
    return jnp.asarray(np.stack(masks).astype(np.float32))


def _hgrn_rec_kernel(q_ref, k_ref, lf_ref, v_ref, g_ref, ng_ref, masks_ref, o_ref, st_ref, *, chunk, dh):
    hg = pl.program_id(1)
    heads = q_ref.shape[1] // dh
    levels = _hgrn_levels(chunk)

    @pl.when(pl.program_id(2) == 0)
    def _():
        st_ref[...] = jnp.zeros_like(st_ref)

    def head_chunk(sl, hh):
        ln = slice(hh * dh, (hh + 1) * dh)
        q, k, lf, v, g = q_ref[sl, ln], k_ref[sl, ln], lf_ref[sl, ln], v_ref[sl, ln], g_ref[sl, ln]
        b = _cumsum_rows(lf)
        s = masks_ref[len(levels)] * _dot_nt(q.astype(BF16), k.astype(BF16))
        for li, m in enumerate(levels):
            e = _level_decay(b, lf, m)
            s = s + masks_ref[li] * _dot_nt((q * e).astype(BF16), (k * e).astype(BF16))
        o = _dot(s.astype(BF16), v)
        st = st_ref[hh]
        b_last = b[chunk - 1:chunk, :]
        o = o + _dot_nt((q * jnp.exp(b)).astype(BF16), st.astype(BF16))
        kt = (k * jnp.exp(b_last - b)).astype(BF16)
        st_ref[hh] = st * jnp.exp(b_last) + _dot_tn(v, kt)
        ms = jnp.mean(o * o, axis=-1, keepdims=True)
        o = o * lax.rsqrt(ms + RMS_EPS) * ng_ref[pl.ds(hg * heads + hh, 1), :]
        o_ref[sl, ln] = (o * (g * jax.nn.sigmoid(g))).astype(o_ref.dtype)

    def body(c, carry):
        sl = pl.ds(pl.multiple_of(c * chunk, chunk), chunk)
        for hh in range(heads):
            head_chunk(sl, hh)
        return carry

    lax.fori_loop(0, q_ref.shape[0] // chunk, body, 0)


def _hgrn_rec(q, k, lf, v, g, norm_g, batch, seq):
    n, d = q.shape
    dh = d // HG_HEADS
    tt = _tile(seq, HG_TIME_TILE)
    chunk = _tile(tt, HG_CHUNK)
    nt = seq // tt
    masks = _hgrn_masks(chunk)
    blk = pl.BlockSpec((tt, HG_HEAD_GROUP * dh), lambda b, h, t: (b * nt + t, h))
    return pl.pallas_call(
        functools.partial(_hgrn_rec_kernel, chunk=chunk, dh=dh),
        grid=(batch, HG_HEADS // HG_HEAD_GROUP, nt),
        in_specs=[blk, blk, blk, blk, blk, _const_spec(norm_g.shape), _const_spec(masks.shape)],
        out_specs=blk,
        out_shape=jax.ShapeDtypeStruct((n, d), BF16),
        scratch_shapes=[pltpu.VMEM((HG_HEAD_GROUP, dh, dh), F32)],
        compiler_params=_cparams("parallel", "parallel", "arbitrary"),
        name="hgrn_rec",
    )(q, k, lf, v, g, norm_g, masks)


def _proj_ln_kernel(a_ref, w_ref, x_ref, g_ref, b_ref, o_ref, *, alpha):
    y = _dot(a_ref[...], w_ref[...])
    o_ref[...] = _layer_norm(alpha * x_ref[...] + y, g_ref[...], b_ref[...])


def _proj_ln(a, w, x, g, b, alpha):
    n, d = x.shape
    ka = a.shape[1]
    tm = _tile(n, ROW_TILE)
    return pl.pallas_call(
        functools.partial(_proj_ln_kernel, alpha=alpha),
        grid=(n // tm,),
        in_specs=[_row_spec(tm, ka), _const_spec((ka, d)), _row_spec(tm, d), _const_spec((1, d)),
                  _const_spec((1, d))],
        out_specs=_row_spec(tm, d),
        out_shape=jax.ShapeDtypeStruct((n, d), F32),
        compiler_params=_cparams("parallel"),
        name="proj_ln",
    )(a, w, x, g, b)


def _swiglu_step(xb, wg_ref, wu_ref, wd_ref):
    gate = _dot(xb, wg_ref[...])
    up = _dot(xb, wu_ref[...])
    act = (gate * jax.nn.sigmoid(gate) * up).astype(BF16)
    return _dot(act, wd_ref[...])


def _ffn_ln_kernel(x_ref, wg_ref, wu_ref, wd_ref, g_ref, b_ref, o_ref, acc_ref, *, alpha):
    f = pl.program_id(1)
    y = _swiglu_step(x_ref[...].astype(BF16), wg_ref, wu_ref, wd_ref)

    @pl.when(f == 0)
    def _():
        acc_ref[...] = y

    @pl.when(f > 0)
    def _():
        acc_ref[...] += y

    @pl.when(f == pl.num_programs(1) - 1)
    def _():
        o_ref[...] = _layer_norm(alpha * x_ref[...] + acc_ref[...], g_ref[...], b_ref[...])


def _ffn_ln(x, w_gate_up, w_down, layer, g, b, alpha):
    n, d = x.shape
    ff = w_down.shape[1]
    tm = _tile(n, FFN_ROW_TILE)
    tf = _tile(ff, FFN_F_TILE)
    nf = ff // tf
    return pl.pallas_call(
        functools.partial(_ffn_ln_kernel, alpha=alpha),
        grid=(n // tm, nf),
        in_specs=[pl.BlockSpec((tm, d), lambda i, f: (i, 0)),
                  pl.BlockSpec((None, d, tf), lambda i, f: (layer, 0, f)),
                  pl.BlockSpec((None, d, tf), lambda i, f: (layer, 0, f + nf)),
                  pl.BlockSpec((None, tf, d), lambda i, f: (layer, f, 0)),
                  _const_spec((1, d)), _const_spec((1, d))],
        out_specs=pl.BlockSpec((tm, d), lambda i, f: (i, 0)),
        out_shape=jax.ShapeDtypeStruct((n, d), F32),
        scratch_shapes=[pltpu.VMEM((tm, d), F32)],
        compiler_params=_cparams("parallel", "arbitrary"),
        name="ffn_ln",
    )(x, w_gate_up, w_gate_up, w_down, g, b)


def _rows_from_tiles(t_ref):
    return jnp.concatenate([t_ref[:, c, :] for c in range(t_ref.shape[1])], axis=1)


def _rows_to_tiles(t_ref, y):
    for c in range(t_ref.shape[1]):
        t_ref[:, c, :] = y[:, c * LANES:(c + 1) * LANES]


def _moe_ffn_kernel(te_ref, nt_ref, xs_hbm, wg_ref, wu_ref, wd_ref, ys_hbm,
                    xrow_ref, xb_ref, acc_ref, ybuf_ref, in_sem, out_sem):
    i, f = pl.program_id(0), pl.program_id(1)
    n_live = nt_ref[0]
    live = i < n_live
    last = f == pl.num_programs(1) - 1
    tm, d = acc_ref.shape

    def fetch(tile, slot):
        return [pltpu.make_async_copy(xs_hbm.at[pl.ds(tile * tm, tm), c, :],
                                      xrow_ref.at[slot, :, pl.ds(c * LANES, LANES)], in_sem.at[slot])
                for c in range(d // LANES)]

    def write(tile):
        return [pltpu.make_async_copy(ybuf_ref.at[:, pl.ds(c * LANES, LANES)],
                                      ys_hbm.at[pl.ds(tile * tm, tm), c, :], out_sem)
                for c in range(d // LANES)]

    @pl.when(live & (f == 0))
    def _():
        @pl.when(i == 0)
        def _():
            for cp in fetch(0, 0):
                cp.start()

        @pl.when(i + 1 < n_live)
        def _():
            for cp in fetch(i + 1, (i + 1) % 2):
                cp.start()

        for cp in fetch(i, i % 2):
            cp.wait()
        xb_ref[...] = xrow_ref[i % 2].astype(BF16)

    def flush(y_rows):
        @pl.when(i > 0)
        def _():
            for cp in write(i - 1):
                cp.wait()

        ybuf_ref[...] = y_rows()
        for cp in write(i):
            cp.start()

    @pl.when(live)
    def _():
        y = _swiglu_step(xb_ref[...], wg_ref, wu_ref, wd_ref)

        @pl.when(f == 0)
        def _():
            acc_ref[...] = y

        @pl.when((f > 0) & jnp.logical_not(last))
        def _():
            acc_ref[...] += y

        @pl.when(last)
        def _():
            flush(lambda: acc_ref[...] + y)

    @pl.when(jnp.logical_not(live) & last)
    def _():
        flush(lambda: jnp.zeros(ybuf_ref.shape, F32))

    @pl.when(last & (i == pl.num_programs(0) - 1))
    def _():
        for cp in write(i):
            cp.wait()


def _moe_ffn(tile_expert, n_live, xs, w_gate_up, w_down, layer, tm):
    p, sub, lanes = xs.shape
    d = sub * lanes
    ff = w_down.shape[2]
    assert p % tm == 0
    tf = _tile(ff, FFN_F_TILE)
    nf = ff // tf
    assert nf > 1
    grid_spec = pltpu.PrefetchScalarGridSpec(
        num_scalar_prefetch=2,
        grid=(p // tm, nf),
        in_specs=[pl.BlockSpec(memory_space=pl.ANY),
                  pl.BlockSpec((None, None, d, tf), lambda i, f, te, nt: (layer, te[i], 0, f)),
                  pl.BlockSpec((None, None, d, tf), lambda i, f, te, nt: (layer, te[i], 0, f + nf)),
                  pl.BlockSpec((None, None, tf, d), lambda i, f, te, nt: (layer, te[i], f, 0))],
        out_specs=pl.BlockSpec(memory_space=pl.ANY),
        scratch_shapes=[pltpu.VMEM((2, tm, d), F32), pltpu.VMEM((tm, d), BF16), pltpu.VMEM((tm, d), F32),
                        pltpu.VMEM((tm, d), F32), pltpu.SemaphoreType.DMA((2,)), pltpu.SemaphoreType.DMA(())],
    )
    return pl.pallas_call(
        _moe_ffn_kernel,
        grid_spec=grid_spec,
        out_shape=jax.ShapeDtypeStruct(xs.shape, F32),
        compiler_params=_cparams("arbitrary", "arbitrary"),
        name="moe_ffn",
    )(tile_expert, n_live, xs, w_gate_up, w_gate_up, w_down)


def _router_kernel(x_ref, w_ref, tril_ref, gate_ref, idx_ref, cnt_ref, carry_ref):
    @pl.when(pl.program_id(0) == 0)
    def _():
        carry_ref[...] = jnp.zeros_like(carry_ref)

    logits = jnp.dot(x_ref[...], w_ref[...], preferred_element_type=F32, precision=lax.Precision.HIGHEST)
    lane = lax.broadcasted_iota(I32, logits.shape, 1)
    l1 = jnp.where(lane < N_EXPERTS, logits, -jnp.inf)
    m1 = jnp.max(l1, axis=-1, keepdims=True)
    i1 = jnp.min(jnp.where(l1 == m1, lane, LANES), axis=-1, keepdims=True)
    l2 = jnp.where(lane == i1, -jnp.inf, l1)
    m2 = jnp.max(l2, axis=-1, keepdims=True)
    i2 = jnp.min(jnp.where(l2 == m2, lane, LANES), axis=-1, keepdims=True)
    e = jnp.exp(m2 - m1)
    den = 1.0 + e
    gate_ref[...] = jnp.where(lane == 0, 1.0 / den, jnp.where(lane == 1, e / den, 0.0))
    oh1 = jnp.where(lane == i1, 1.0, 0.0)
    oh2 = jnp.where(lane == i2, 1.0, 0.0)
    oh = oh1 + oh2
    before = _dot(tril_ref[...], oh.astype(BF16)) + carry_ref[...]
    r1 = jnp.sum(oh1 * before, axis=-1, keepdims=True).astype(I32)
    r2 = jnp.sum(oh2 * before, axis=-1, keepdims=True).astype(I32)
    carry_ref[...] += jnp.sum(oh, axis=0, keepdims=True)
    cnt_ref[...] = carry_ref[...]
    idx_ref[...] = jnp.where(lane == 0, i1, jnp.where(lane == 1, i2, jnp.where(lane == 2, r1, jnp.where(
        lane == 3, r2, 0))))


def _router(x, w_router_pad):
    n, d = x.shape
    tm = _tile(n, ROW_TILE)
    tril = jnp.asarray(np.tril(np.ones((tm, tm), np.float32), -1), BF16)
    return pl.pallas_call(
        _router_kernel,
        grid=(n // tm,),
        in_specs=[_row_spec(tm, d), _const_spec((d, LANES)), _const_spec((tm, tm))],
        out_specs=[_row_spec(tm, LANES), _row_spec(tm, LANES), _const_spec((1, LANES))],
        out_shape=[jax.ShapeDtypeStruct((n, LANES), F32), jax.ShapeDtypeStruct((n, LANES), I32),
                   jax.ShapeDtypeStruct((1, LANES), F32)],
        scratch_shapes=[pltpu.VMEM((1, LANES), F32)],
        compiler_params=_cparams("arbitrary"),
        name="router",
    )(x, w_router_pad, tril)


def _moe_dispatch_kernel(p1_ref, p2_ref, x_ref, xs_init_hbm, xs_hbm, stage_ref, sem):
    del xs_init_hbm
    td = x_ref.shape[0]
    base = pl.program_id(0) * td
    _rows_to_tiles(stage_ref, x_ref[...])

    def copies(r, row1, row2):
        return (pltpu.make_async_copy(stage_ref.at[r], xs_hbm.at[row1], sem),
                pltpu.make_async_copy(stage_ref.at[r], xs_hbm.at[row2], sem))

    def issue(r, c):
        for cp in copies(r, p1_ref[base + r], p2_ref[base + r]):
            cp.start()
        return c

    def drain(r, c):
        for cp in copies(r, 0, 0):
            cp.wait()
        return c

    lax.fori_loop(0, td, issue, 0, unroll=DMA_LOOP_UNROLL)
    lax.fori_loop(0, td, drain, 0, unroll=DMA_LOOP_UNROLL)


def _moe_dispatch(p1, p2, x, p_rows):
    n, d = x.shape
    sub = d // LANES
    td = _tile(n, GATHER_TILE)
    xs_init = jnp.zeros((p_rows, sub, LANES), F32)
    grid_spec = pltpu.PrefetchScalarGridSpec(
        num_scalar_prefetch=2,
        grid=(n // td,),
        in_specs=[pl.BlockSpec((td, d), lambda i, a, c: (i, 0)), pl.BlockSpec(memory_space=pl.ANY)],
        out_specs=pl.BlockSpec(memory_space=pl.ANY),
        scratch_shapes=[pltpu.VMEM((td, sub, LANES), F32), pltpu.SemaphoreType.DMA(())],
    )
    return pl.pallas_call(
        _moe_dispatch_kernel,
        grid_spec=grid_spec,
        out_shape=jax.ShapeDtypeStruct(xs_init.shape, F32),
        input_output_aliases={3: 0},
        compiler_params=_cparams("arbitrary"),
        name="moe_dispatch",
    )(p1, p2, x, xs_init)


def _moe_combine_ln_kernel(p1_ref, p2_ref, x_ref, gate_ref, ys_hbm, g_ref, b_ref, o_ref, buf_ref, sem, *, alpha):
    i, nsteps = pl.program_id(0), pl.num_programs(0)
    tc = o_ref.shape[0]

    def copies(slot, r, row1, row2):
        return (pltpu.make_async_copy(ys_hbm.at[row1], buf_ref.at[slot, 0, r], sem.at[slot]),
                pltpu.make_async_copy(ys_hbm.at[row2], buf_ref.at[slot, 1, r], sem.at[slot]))

    def issue(step, slot):
        def body(r, c):
            for cp in copies(slot, r, p1_ref[step * tc + r], p2_ref[step * tc + r]):
                cp.start()
            return c

        lax.fori_loop(0, tc, body, 0, unroll=DMA_LOOP_UNROLL)

    @pl.when(i == 0)
    def _():
        issue(0, 0)

    @pl.when(i + 1 < nsteps)
    def _():
        issue(i + 1, (i + 1) % 2)

    slot = i % 2

    def drain(r, c):
        for cp in copies(slot, r, 0, 0):
            cp.wait()
        return c

    lax.fori_loop(0, tc, drain, 0, unroll=DMA_LOOP_UNROLL)
    w = gate_ref[...]
    y = w[:, 0:1] * _rows_from_tiles(buf_ref.at[slot, 0]) + w[:, 1:2] * _rows_from_tiles(buf_ref.at[slot, 1])
    o_ref[...] = _layer_norm(alpha * x_ref[...] + y, g_ref[...], b_ref[...])


def _moe_combine_ln(p1, p2, x, gate, ys, g, b, alpha):
    n, d = x.shape
    sub = d // LANES
    tc = _tile(n, COMBINE_TILE)
    grid_spec = pltpu.PrefetchScalarGridSpec(
        num_scalar_prefetch=2,
        grid=(n // tc,),
        in_specs=[pl.BlockSpec((tc, d), lambda i, a, c: (i, 0)),
                  pl.BlockSpec((tc, LANES), lambda i, a, c: (i, 0)),
                  pl.BlockSpec(memory_space=pl.ANY),
                  pl.BlockSpec((1, d), lambda i, a, c: (0, 0)),
                  pl.BlockSpec((1, d), lambda i, a, c: (0, 0))],
        out_specs=pl.BlockSpec((tc, d), lambda i, a, c: (i, 0)),
        scratch_shapes=[pltpu.VMEM((2, 2, tc, sub, LANES), F32), pltpu.SemaphoreType.DMA((2,))],
    )
    return pl.pallas_call(
        functools.partial(_moe_combine_ln_kernel, alpha=alpha),
        grid_spec=grid_spec,
        out_shape=jax.ShapeDtypeStruct((n, d), F32),
        compiler_params=_cparams("arbitrary"),
        name="moe_combine_ln",
    )(p1, p2, x, gate, ys, g, b)


def _moe_layer(x, w_router_pad, w_gate_up, w_down, layer, g, b, alpha):
    n = x.shape[0]
    gate, idx, cnt = _router(x, w_router_pad)
    tm = min(MOE_ROW_TILE, 2 * n)
    p_rows = 2 * n + N_EXPERTS * tm
    counts = cnt[0, :N_EXPERTS].astype(I32)
    padded = ((counts + tm - 1) // tm) * tm
    pends = jnp.cumsum(padded)
    pstarts = pends - padded
    p1 = pstarts[idx[:, 0]] + idx[:, 2]
    p2 = pstarts[idx[:, 1]] + idx[:, 3]
    tile_start = jnp.arange(p_rows // tm, dtype=I32) * tm
    tile_expert = jnp.minimum(jnp.sum(tile_start[:, None] >= pends[None, :], axis=1), N_EXPERTS - 1).astype(I32)
    n_live = (pends[-1] // tm).astype(I32).reshape(1)
    xs = _moe_dispatch(p1, p2, x, p_rows)
    ys = _moe_ffn(tile_expert, n_live, xs, w_gate_up, w_down, layer, tm)
    return _moe_combine_ln(p1, p2, x, gate, ys, g, b, alpha)


def _rms_norm(x, g):
    ms = jnp.mean(x * x, axis=-1, keepdims=True)
    return x * lax.rsqrt(ms + RMS_EPS) * g


def _dsa_proj_kernel(x_ref, win_ref, qg_ref, kvg_ref, kig_ref, kib_ref, wuq_ref, wuk_ref, wqi_ref,
                     qlat_ref, qidx_ref, ckv_ref, misc_ref, *, w_scale):
    xb = x_ref[...].astype(BF16)
    qr, kr = DSA_Q_RANK, DSA_KV_RANK
    cq = _rms_norm(_dot(xb, win_ref[:, 0:qr]), qg_ref[...]).astype(BF16)
    ckv_ref[...] = _rms_norm(_dot(xb, win_ref[:, qr:qr + kr]), kvg_ref[...]).astype(ckv_ref.dtype)
    misc = _dot(xb, win_ref[:, qr + kr:qr + kr + LANES])
    lane = lax.broadcasted_iota(I32, misc.shape, 1)
    is_key = lane < IDX_DIM
    mu = jnp.sum(jnp.where(is_key, misc, 0.0), axis=-1, keepdims=True) * (1.0 / IDX_DIM)
    kc = jnp.where(is_key, misc - mu, 0.0)
    var = jnp.sum(kc * kc, axis=-1, keepdims=True) * (1.0 / IDX_DIM)
    kn = kc * lax.rsqrt(var + LN_EPS) * kig_ref[...] + kib_ref[...]
    misc_ref[...] = jnp.where(is_key, kn, misc * w_scale)
    q = _dot(cq, wuq_ref[...])
    hd = DSA_HEAD_DIM
    for h in range(DSA_HEADS):
        qidx_ref[h] = _dot(cq, wqi_ref[:, h * LANES:(h + 1) * LANES]).astype(qidx_ref.dtype)
        qlat_ref[h] = _dot(q[:, h * hd:(h + 1) * hd].astype(BF16), wuk_ref[h]).astype(qlat_ref.dtype)


def _head_rows_spec(t, wd):
    return pl.BlockSpec((DSA_HEADS, t, wd), lambda i: (0, i, 0))


def _dsa_proj(x, w_in_pad, q_g, kv_g, ki_g, ki_b, w_uq, w_uk, w_qidx_pad):
    n, d = x.shape
    tm = _tile(n, ROW_TILE)
    w_scale = IDX_HEADS ** -0.5 * IDX_DIM ** -0.5
    outs = [jax.ShapeDtypeStruct((DSA_HEADS, n, DSA_KV_RANK), BF16),
            jax.ShapeDtypeStruct((IDX_HEADS, n, LANES), BF16),
            jax.ShapeDtypeStruct((n, DSA_KV_RANK), BF16),
            jax.ShapeDtypeStruct((n, LANES), F32)]
    ins = [x, w_in_pad, q_g, kv_g, ki_g, ki_b, w_uq, w_uk, w_qidx_pad]
    return pl.pallas_call(
        functools.partial(_dsa_proj_kernel, w_scale=w_scale),
        grid=(n // tm,),
        in_specs=[_row_spec(tm, d)] + [_const_spec(a.shape) for a in ins[1:]],
        out_specs=[_head_rows_spec(tm, DSA_KV_RANK), _head_rows_spec(tm, LANES), _row_spec(tm, DSA_KV_RANK),
                   _row_spec(tm, LANES)],
        out_shape=outs,
        compiler_params=_cparams("parallel"),
        name="dsa_proj",
    )(*ins)


def _order_key(x):
    k = pltpu.bitcast(x, I32)
    k = k ^ ((k >> 31) & 0x7FFFFFFF)
    return jnp.where(x == 0.0, 0, k)


def _dsa_attn_kernel(qidx_ref, wt_ref, qlat_ref, kidx_ref, ckv_t_ref, ckv_ref, o_ref,
                     key_ref, plane_ref, bias_ref, s_ref, p_ref, m_ref, l_ref, alpha_ref, acc_ref,
                     *, topk, scale, tk, rc, rg):
    nh, tq, kr = qlat_ref.shape
    seq = ckv_ref.shape[0]
    q0 = pl.program_id(1) * tq
    n_tiles = (q0 + tq + tk - 1) // tk
    int_min = jnp.iinfo(jnp.int32).min
    wr = tk // WORD_BITS
    nbits = max(1, (seq - 1).bit_length())
    w_t = wt_ref[...]
    q_lane = q0 + lax.broadcasted_iota(I32, (tk, tq), 1)
    key_row = lax.broadcasted_iota(I32, (tk, tq), 0)
    sub = lax.broadcasted_iota(I32, (SUBLANES, tq), 0)

    def key_tile(kb):
        return pl.ds(pl.multiple_of(kb * tk, tk), tk)

    q_idx_all = qidx_ref[...].reshape(nh * tq, LANES)

    def score_tile(kb, c):
        r = _dot_nt(kidx_ref[key_tile(kb), :], q_idx_all)
        score = jnp.zeros((tk, tq), F32)
        for h in range(nh):
            score = score + jnp.maximum(r[:, h * tq:(h + 1) * tq], 0.0) * w_t[h:h + 1, :]
        score = jnp.where(kb * tk + key_row <= q_lane, score, NEG_BIG)
        key = _order_key(score)
        key_ref[key_tile(kb), :] = key
        u = key ^ int_min
        slabs = [u[j * wr:(j + 1) * wr, :] for j in range(WORD_BITS)]
        step, mask = WORD_BITS // 2, (1 << (WORD_BITS // 2)) - 1
        while step:
            for lo in range(WORD_BITS):
                if lo & step == 0:
                    hi = lo + step
                    d = (slabs[lo] ^ lax.shift_right_logical(slabs[hi], jnp.full_like(slabs[hi], step))) & mask
                    slabs[lo] = slabs[lo] ^ d
                    slabs[hi] = slabs[hi] ^ (d << step)
            step //= 2
            if step:
                mask = mask ^ (mask << step)
        for b in range(WORD_BITS):
            plane_ref[WORD_BITS - 1 - b, pl.ds(pl.multiple_of(kb * wr, wr), wr), :] = slabs[b]
        return c

    plane_ref[...] = jnp.zeros(plane_ref.shape, I32)
    lax.fori_loop(0, n_tiles, score_tile, 0)

    def count_keys(pred):
        n_acc = 4

        def tile(kb, accs):
            blk = key_ref[key_tile(kb), :]
            accs = list(accs)
            for s8 in range(tk // SUBLANES):
                hit = pred(blk[s8 * SUBLANES:(s8 + 1) * SUBLANES, :], kb * tk + s8 * SUBLANES)
                accs[s8 % n_acc] = accs[s8 % n_acc] + jnp.where(hit, 1.0, 0.0)
            return tuple(accs)

        accs = lax.fori_loop(0, n_tiles, tile, tuple(jnp.zeros((SUBLANES, tq), F32) for _ in range(n_acc)))
        return jnp.sum((accs[0] + accs[1]) + (accs[2] + accs[3]), axis=0, keepdims=True)

    def rep(row):
        return jnp.broadcast_to(row, (SUBLANES, tq))

    def bit_step(it, carry):
        cands, above, t = carry
        sig = WORD_BITS - 1 - it
        ones = cands & plane_ref[sig]
        n_ones = jnp.sum(lax.population_count(ones), axis=0, keepdims=True)
        take = above + n_ones >= topk
        cands = jnp.where(take, ones, cands ^ ones)
        above = jnp.where(take, above, above + n_ones)
        t = t | jnp.where(take, lax.shift_left(jnp.int32(1), sig), 0)
        return cands, above, t

    cands, above, t_u = lax.fori_loop(
        0, WORD_BITS, bit_step,
        (jnp.full((seq // WORD_BITS, tq), -1, I32), jnp.zeros((1, tq), I32), jnp.zeros((1, tq), I32)))
    t_all = t_u ^ int_min
    t_b = rep(t_all)
    need = (topk - above).astype(F32)
    extra = jnp.sum(lax.population_count(cands), axis=0, keepdims=True).astype(F32) - need

    def last_kept_tie():
        def index_step(it, j):
            cand = j + lax.shift_left(jnp.int32(1), nbits - 1 - it)
            cand_b = rep(cand)
            below = count_keys(lambda k, r0: (k == t_b) & (r0 + sub < cand_b))
            return jnp.where(below < need, cand, j)

        return lax.fori_loop(0, nbits, index_step, jnp.zeros((1, tq), I32))

    j_all = lax.cond(jnp.max(extra) > 0.0, last_kept_tie, lambda: jnp.full((1, tq), seq, I32))
    q_lat_all = qlat_ref[...].reshape(nh * tq, kr)
    m_ref[...] = jnp.full(m_ref.shape, NEG_BIG, F32)
    l_ref[...] = jnp.zeros(l_ref.shape, F32)
    acc_ref[...] = jnp.zeros(acc_ref.shape, F32)
    c2 = scale * LOG2_E

    def attn_tile(kb, c):
        key = key_ref[key_tile(kb), :]
        krow = kb * tk + key_row
        sel = ((key > t_all) | ((key == t_all) & (krow <= j_all))) & (krow <= q_lane)
        bias_ref[...] = jnp.where(sel, 0.0, NEG_BIG).T
        ckv_t = ckv_t_ref[:, key_tile(kb)]
        ckv = ckv_ref[key_tile(kb), :]
        for g0 in range(0, nh * tq, rg):
            grp = slice(g0, g0 + rg)
            s_ref[grp, :] = _dot(q_lat_all[grp, :], ckv_t)
            for r0 in range(g0, g0 + rg, rc):
                rows = slice(r0, r0 + rc)
                s = s_ref[rows, :] + bias_ref[r0 % tq:r0 % tq + rc, :]
                m_prev = m_ref[rows, :]
                m_new = jnp.maximum(m_prev, jnp.max(s, axis=-1, keepdims=True))
                alpha = jnp.exp2((m_prev - m_new) * c2)
                p_lanes = None
                for g in range(tk // LANES):
                    lanes = slice(g * LANES, (g + 1) * LANES)
                    p = jnp.exp2((s[:, lanes] - m_new) * c2)
                    p_ref[rows, lanes] = p.astype(BF16)
                    p_lanes = p if p_lanes is None else p_lanes + p
                l_ref[rows, :] = alpha * l_ref[rows, :] + p_lanes
                m_ref[rows, :] = m_new
                alpha_ref[rows, :] = alpha
            pv = _dot(p_ref[grp, :], ckv)
            a = alpha_ref[grp, :]
            for g in range(kr // LANES):
                lanes = slice(g * LANES, (g + 1) * LANES)
                acc_ref[grp, lanes] = a * acc_ref[grp, lanes] + pv[:, lanes]
        return c

    lax.fori_loop(0, n_tiles, attn_tile, 0)
    den = jnp.sum(l_ref[...], axis=-1, keepdims=True)
    o_ref[...] = (acc_ref[...] / den).reshape(nh, tq, kr).astype(o_ref.dtype)


def _dsa_attn(q_idx, w_t, q_lat, kidx, ckv_t, ckv, batch, seq):
    nh, n, kr = q_lat.shape
    tq = _tile(seq, DSA_Q_TILE)
    tk = _tile(seq, DSA_KEY_TILE)
    rc = _tile(tq, DSA_SOFTMAX_ROWS)
    nq = seq // tq
    topk = min(IDX_TOPK_MAX, seq // 4)
    heads = lambda wd: pl.BlockSpec((nh, tq, wd), lambda b, i: (0, b * nq + i, 0))
    stat = pltpu.VMEM((nh * tq, LANES), F32)
    return pl.pallas_call(
        functools.partial(_dsa_attn_kernel, topk=topk, scale=DSA_HEAD_DIM ** -0.5, tk=tk, rc=rc,
                          rg=_tile(nh * tq, DSA_GROUP_ROWS)),
        grid=(batch, nq),
        in_specs=[heads(LANES), pl.BlockSpec((None, IDX_HEADS, tq), lambda b, i: (b, 0, i)), heads(kr),
                  pl.BlockSpec((seq, LANES), lambda b, i: (b, 0)),
                  pl.BlockSpec((None, kr, seq), lambda b, i: (b, 0, 0)),
                  pl.BlockSpec((seq, kr), lambda b, i: (b, 0))],
        out_specs=heads(kr),
        out_shape=jax.ShapeDtypeStruct((nh, n, kr), BF16),
        scratch_shapes=[pltpu.VMEM((seq, tq), I32), pltpu.VMEM((WORD_BITS, seq // WORD_BITS, tq), I32),
                        pltpu.VMEM((tq, tk), F32),
                        pltpu.VMEM((nh * tq, tk), F32), pltpu.VMEM((nh * tq, tk), BF16),
                        stat, stat, stat, pltpu.VMEM((nh * tq, kr), F32)],
        compiler_params=_cparams("parallel", "parallel"),
        name="dsa_attn",
    )(q_idx, w_t, q_lat, kidx, ckv_t, ckv)


def _dsa_out_ln_kernel(olat_ref, wuv_ref, wout_ref, x_ref, g_ref, b_ref, o_ref, *, alpha):
    hd = DSA_HEAD_DIM
    y = None
    for h in range(DSA_HEADS):
        oh = _dot(olat_ref[h], wuv_ref[h]).astype(BF16)
        yh = _dot(oh, wout_ref[h * hd:(h + 1) * hd, :])
        y = yh if y is None else y + yh
    o_ref[...] = _layer_norm(alpha * x_ref[...] + y, g_ref[...], b_ref[...])


def _dsa_out_ln(o_lat, w_uv, w_out, x, g, b, alpha):
    n, d = x.shape
    tm = _tile(n, ROW_TILE)
    return pl.pallas_call(
        functools.partial(_dsa_out_ln_kernel, alpha=alpha),
        grid=(n // tm,),
        in_specs=[_head_rows_spec(tm, o_lat.shape[2]), _const_spec(w_uv.shape), _const_spec(w_out.shape),
                  _row_spec(tm, d), _const_spec((1, d)), _const_spec((1, d))],
        out_specs=_row_spec(tm, d),
        out_shape=jax.ShapeDtypeStruct((n, d), F32),
        compiler_params=_cparams("parallel"),
        name="dsa_out_ln",
    )(o_lat, w_uv, w_out, x, g, b)


def _dsa_layer(x, batch, seq, w_in, q_g, kv_g, w_uq, w_uk, w_uv, w_qidx, ki_g, ki_b, w_out, g, b, alpha):
    d = x.shape[1]
    qk = DSA_Q_RANK + DSA_KV_RANK
    tail = w_in.shape[1] - qk
    w_in_pad = jnp.pad(w_in, ((0, 0), (0, LANES - tail))).astype(BF16)
    w_qidx_pad = jnp.pad(w_qidx.reshape(DSA_Q_RANK, IDX_HEADS, IDX_DIM),
                         ((0, 0), (0, 0), (0, LANES - IDX_DIM))).reshape(DSA_Q_RANK, IDX_HEADS * LANES)
    pad_lane = lambda v: jnp.pad(v, (0, LANES - v.shape[0])).reshape(1, LANES)
    q_lat, q_idx, ckv, misc = _dsa_proj(
        x, w_in_pad, q_g.reshape(1, -1), kv_g.reshape(1, -1), pad_lane(ki_g), pad_lane(ki_b),
        w_uq.astype(BF16), w_uk.astype(BF16), w_qidx_pad.astype(BF16))
    lane = jnp.arange(LANES)
    kidx = jnp.where(lane < IDX_DIM, misc, 0.0).astype(BF16)
    w_t = misc[:, IDX_DIM:IDX_DIM + IDX_HEADS].reshape(batch, seq, IDX_HEADS).transpose(0, 2, 1)
    ckv_t = ckv.reshape(batch, seq, DSA_KV_RANK).transpose(0, 2, 1)
    o_lat = _dsa_attn(q_idx, w_t, q_lat, kidx, ckv_t, ckv, batch, seq)
    return _dsa_out_ln(o_lat, w_uv.astype(BF16), w_out.astype(BF16), x, g, b, alpha)


def kernel(x, ln_g, ln_b, hg_w_in, hg_lower_bounds, hg_norm_g, hg_w_out, dsa_w_in, dsa_q_norm_g, dsa_kv_norm_g, dsa_w_uq, dsa_w_uk, dsa_w_uv, dsa_w_qidx, dsa_kidx_norm_g, dsa_kidx_norm_b, dsa_w_out, ffn_w_gate_up, ffn_w_down, moe_w_router, moe_w_gate_up, moe_w_down):
    batch, seq, d = x.shape
    depth = ln_g.shape[0]
    alpha = (2 * depth) ** 0.25
    lb_all = jax.nn.softmax(hg_lower_bounds.astype(F32), axis=0)
    lb_all = jnp.cumsum(lb_all, axis=0) - lb_all[0]
    h = x.reshape(batch * seq, d)
    row = lambda v: v.reshape(1, d)
    ffn_gu, ffn_dn = ffn_w_gate_up.astype(BF16), ffn_w_down.astype(BF16)
    moe_gu, moe_dn = moe_w_gate_up.astype(BF16), moe_w_down.astype(BF16)
    for layer in range(depth):
        j = layer // 2
        g0, b0, g1, b1 = row(ln_g[layer, 0]), row(ln_b[layer, 0]), row(ln_g[layer, 1]), row(ln_b[layer, 1])
        if layer % 2 == 0:
            q, k, lf, v, g = _hgrn_proj(h, hg_w_in[j].astype(BF16), row(lb_all[j]))
            o = _hgrn_rec(q, k, lf, v, g, hg_norm_g[j], batch, seq)
            h = _proj_ln(o, hg_w_out[j].astype(BF16), h, g0, b0, alpha)
            h = _ffn_ln(h, ffn_gu, ffn_dn, j, g1, b1, alpha)
        else:
            h = _dsa_layer(h, batch, seq, dsa_w_in[j], dsa_q_norm_g[j], dsa_kv_norm_g[j], dsa_w_uq[j],
                           dsa_w_uk[j], dsa_w_uv[j], dsa_w_qidx[j], dsa_kidx_norm_g[j], dsa_kidx_norm_b[j],
                           dsa_w_out[j], g0, b0, alpha)
            w_router_pad = jnp.pad(moe_w_router[j], ((0, 0), (0, LANES - N_EXPERTS)))
            h = _moe_layer(h, w_router_pad, moe_gu, moe_dn, j, g1, b1, alpha)
    return h.reshape(batch, seq, d)
```
